```python
import math
import jax
import jax.numpy as jnp
from jax import lax
import numpy as np

D_MODEL = 1024
BATCH = 8
SEQ = 2048
DEPTH = 4
DEC_BATCH = 32
DEC_SEQ = 4
PAST_LEN = 8192
PAGE_SIZE = 128

HEAD_DIM = 64
MIX_W = D_MODEL
N_HEADS = MIX_W // HEAD_DIM
H_MLSTM = N_HEADS // 4
H_MOBA = N_HEADS // 2
H_HGRN = N_HEADS - H_MLSTM - H_MOBA
W_MLSTM = H_MLSTM * HEAD_DIM
W_MOBA = H_MOBA * HEAD_DIM
W_HGRN = H_HGRN * HEAD_DIM
MLSTM_CHUNK = 64
HGRN_CHUNK = 64
MOBA_BLOCK = 256
MOBA_TOPK = 3
MOBA_Q_BLOCK = 32
ROPE_THETA = 10000.0
D_FF = ((8 * D_MODEL) // 3 + 127) // 128 * 128
CONV_W = 3
EPS = 1e-6
_IN_WIDTHS = (W_MLSTM,) * 4 + (H_MLSTM,) * 2 + (W_MOBA,) * 3 + (W_HGRN,) * 4
IN_W = sum(_IN_WIDTHS)
IN_SPLITS = tuple(int(s) for s in np.cumsum(_IN_WIDTHS)[:-1])

kernel_name = "hymba_mlstm_moba_hgrn2_step"


def rmsnorm(x, w):
    xf = x.astype(jnp.float32)
    y = xf * lax.rsqrt(jnp.mean(xf * xf, axis=-1, keepdims=True) + EPS)
    return (y * w.astype(jnp.float32)).astype(x.dtype)


def rope(x, pos):
    half = HEAD_DIM // 2
    inv = ROPE_THETA ** (-jnp.arange(half, dtype=jnp.float32) / half)
    ang = pos.astype(jnp.float32)[:, None] * inv[None, :]
    cos = jnp.cos(ang)[None, :, None, :]
    sin = jnp.sin(ang)[None, :, None, :]
    xf = x.astype(jnp.float32)
    x1, x2 = xf[..., :half], xf[..., half:]
    return jnp.concatenate([x1 * cos - x2 * sin, x2 * cos + x1 * sin], axis=-1).astype(x.dtype)


def _to_chunks(a, L):
    B, H, T = a.shape[:3]
    return jnp.moveaxis(a.reshape(B, H, T // L, L, *a.shape[3:]), 2, 0)


def _from_chunks(a):
    N, B, H, L, D = a.shape
    return jnp.moveaxis(a, 0, 2).reshape(B, H, N * L, D).transpose(0, 2, 1, 3)


def mlstm(q, k, v, log_i, log_f, C0, n0, m0):
    T = q.shape[2]
    L = math.gcd(T, MLSTM_CHUNK)
    k = k * (HEAD_DIM ** -0.5)
    causal = jnp.tril(jnp.ones((L, L), dtype=bool))

    def step(carry, xs):
        C, n, m = carry
        qc, kc, vc, ic, fc = xs
        b = jnp.cumsum(fc, axis=-1)
        logD = jnp.where(causal, b[..., :, None] - b[..., None, :] + ic[..., None, :], -jnp.inf)
        m_inter = b + m[..., None]
        m_t = jnp.maximum(m_inter, jnp.max(logD, axis=-1))
        s = jnp.einsum('bhtd,bhsd->bhts', qc, kc) * jnp.exp(logD - m_t[..., None])
        g = jnp.exp(m_inter - m_t)
        num = g[..., None] * jnp.einsum('bhtd,bhde->bhte', qc, C) + jnp.einsum('bhts,bhse->bhte', s, vc)
        den = g * jnp.einsum('bhtd,bhd->bht', qc, n) + jnp.sum(s, axis=-1)
        h = num / jnp.maximum(jnp.abs(den), jnp.exp(-m_t))[..., None]
        bL = b[..., -1]
        log_w = bL[..., None] - b + ic
        m_new = jnp.maximum(bL + m, jnp.max(log_w, axis=-1))
        decay = jnp.exp(bL + m - m_new)
        kw = kc * jnp.exp(log_w - m_new[..., None])[..., None]
        C_new = decay[..., None, None] * C + jnp.einsum('bhsd,bhse->bhde', kw, vc)
        n_new = decay[..., None] * n + jnp.sum(kw, axis=-2)
        return (C_new, n_new, m_new), h

    carry0 = (C0.astype(jnp.float32), n0.astype(jnp.float32), m0.astype(jnp.float32))
    xs = (_to_chunks(q, L), _to_chunks(k, L), _to_chunks(v, L), _to_chunks(log_i, L), _to_chunks(log_f, L))
    (C, n, m), hs = lax.scan(step, carry0, xs)
    return _from_chunks(hs), C, n, m


def hgrn2(q, log_f, i, S0):
    T = q.shape[2]
    L = math.gcd(T, HGRN_CHUNK)
    kk = -jnp.expm1(log_f)
    causal = jnp.tril(jnp.ones((L, L), dtype=bool))

    def step(S, xs):
        qc, gc, kc, vc = xs
        b = jnp.cumsum(gc, axis=2)
        diff = jnp.where(causal[:, :, None], b[:, :, :, None, :] - b[:, :, None, :, :], -jnp.inf)
        A = jnp.einsum('bhtsk,bhsk->bhts', qc[:, :, :, None, :] * jnp.exp(diff), kc)
        o = jnp.einsum('bhtk,bhkv->bhtv', qc * jnp.exp(b), S) + jnp.einsum('bhts,bhsv->bhtv', A, vc)
        bL = b[:, :, -1]
        S_new = jnp.exp(bL)[..., None] * S + jnp.einsum('bhsk,bhsv->bhkv', kc * jnp.exp(bL[:, :, None] - b), vc)
        return S_new, o

    xs = (_to_chunks(q, L), _to_chunks(log_f, L), _to_chunks(kk, L), _to_chunks(i, L))
    S, os_ = lax.scan(step, S0.astype(jnp.float32), xs)
    return _from_chunks(os_), S


def moba(q, k_all, v_all, offset):
    B, Tq, H, D = q.shape
    Tk = k_all.shape[1]
    NB = -(-Tk // MOBA_BLOCK)
    pad = NB * MOBA_BLOCK - Tk
    kp = jnp.pad(k_all, ((0, 0), (0, pad), (0, 0), (0, 0)))
    vp = jnp.pad(v_all, ((0, 0), (0, pad), (0, 0), (0, 0)))
    kb = kp.reshape(B, NB, MOBA_BLOCK, H, D).transpose(0, 3, 1, 2, 4)
    vb = vp.reshape(B, NB, MOBA_BLOCK, H, D).transpose(0, 3, 1, 2, 4)
    kmean = jnp.mean(kb.astype(jnp.float32), axis=3)
    qh = q.transpose(0, 2, 1, 3)
    Qc = math.gcd(Tq, MOBA_Q_BLOCK)
    topk = min(MOBA_TOPK, NB)
    scale = D ** -0.5
    bi = jnp.arange(B)[:, None, None, None]
    hi = jnp.arange(H)[None, :, None, None]

    def one_chunk(c):
        start = c * Qc
        qc = lax.dynamic_slice_in_dim(qh, start, Qc, axis=2).astype(jnp.float32)
        qpos = offset + start + jnp.arange(Qc)
        j = (offset + start) // MOBA_BLOCK
        gate = jnp.einsum('bhqd,bhnd->bhqn', qc, kmean)
        gate = jnp.where(jnp.arange(NB) < j, gate, -jnp.inf)
        _, idx = lax.top_k(gate, topk)
        ks = kb[bi, hi, idx]
        vs = vb[bi, hi, idx]
        s_sel = jnp.einsum('bhqd,bhqnpd->bhqnp', qc, ks) * scale
        s_sel = jnp.where((jnp.arange(topk) < j)[:, None], s_sel, -jnp.inf)
        s_sel = s_sel.reshape(B, H, Qc, topk * MOBA_BLOCK)
        ko = lax.dynamic_slice_in_dim(kb, j, 1, axis=2)[:, :, 0]
        vo = lax.dynamic_slice_in_dim(vb, j, 1, axis=2)[:, :, 0]
        s_own = jnp.einsum('bhqd,bhpd->bhqp', qc, ko) * scale
        kpos = j * MOBA_BLOCK + jnp.arange(MOBA_BLOCK)
        s_own = jnp.where(kpos[None, :] <= qpos[:, None], s_own, -jnp.inf)
        p = jax.nn.softmax(jnp.concatenate([s_sel, s_own], axis=-1), axis=-1)
        p_sel = p[..., :topk * MOBA_BLOCK].reshape(B, H, Qc, topk, MOBA_BLOCK)
        return (jnp.einsum('bhqnp,bhqnpd->bhqd', p_sel, vs)
                + jnp.einsum('bhqp,bhpd->bhqd', p[..., topk * MOBA_BLOCK:], vo))

    outs = lax.map(one_chunk, jnp.arange(Tq // Qc))
    return outs.transpose(1, 0, 3, 2, 4).reshape(B, Tq, H, D)


def trunk_layer(x, offset, k_past, v_past, c0, n0, m0, s0, conv0,
                w_in, m_gate_bias, m_norm, lb, h_norm, w_out, norms, w_up, conv_w, conv_b, w_down):
    B, T, _ = x.shape
    dt = x.dtype
    f32 = jnp.float32
    h = rmsnorm(x, norms[0])
    proj = h @ w_in
    (mq, mk, mv, mo, mi, mf, aq, ak, av, hq, hf, hin, hg) = jnp.split(proj, IN_SPLITS, axis=-1)
    bhtd = lambda a, H: a.reshape(B, T, H, HEAD_DIM).transpose(0, 2, 1, 3).astype(f32)

    log_i = (mi + m_gate_bias[0]).astype(f32).transpose(0, 2, 1)
    log_f = jax.nn.log_sigmoid((mf + m_gate_bias[1]).astype(f32)).transpose(0, 2, 1)
    hA, c_new, n_new, m_new = mlstm(bhtd(mq, H_MLSTM), bhtd(mk, H_MLSTM), bhtd(mv, H_MLSTM),
                                    log_i, log_f, c0, n0, m0)
    oA = jax.nn.sigmoid(mo.astype(f32)).reshape(B, T, H_MLSTM, HEAD_DIM)
    yA = rmsnorm(hA * oA, m_norm.reshape(H_MLSTM, HEAD_DIM)).reshape(B, T, W_MLSTM)

    pos = offset + jnp.arange(T)
    q = rope(aq.reshape(B, T, H_MOBA, HEAD_DIM), pos)
    k = rope(ak.reshape(B, T, H_MOBA, HEAD_DIM), pos)
    v = av.reshape(B, T, H_MOBA, HEAD_DIM)
    if k_past is None:
        k_all, v_all = k, v
    else:
        k_all = jnp.concatenate([k_past.astype(dt), k], axis=1)
        v_all = jnp.concatenate([v_past.astype(dt), v], axis=1)
    yB = moba(q, k_all, v_all, offset).reshape(B, T, W_MOBA)

    lbf = lb.astype(f32)
    log_fh = jnp.logaddexp(jnp.log(lbf), jnp.log1p(-lbf) + jax.nn.log_sigmoid(hf.astype(f32)))
    log_fh = log_fh.reshape(B, T, H_HGRN, HEAD_DIM).transpose(0, 2, 1, 3)
    oC, s_new = hgrn2(bhtd(hq, H_HGRN), log_fh, bhtd(hin, H_HGRN), s0)
    yC = rmsnorm(oC, h_norm.reshape(H_HGRN, HEAD_DIM)).reshape(B, T, W_HGRN) * jax.nn.silu(hg.astype(f32))

    mix = jnp.concatenate([yA, yB, yC], axis=-1).astype(dt) @ w_out
    x = x + rmsnorm(mix, norms[1])

    h = rmsnorm(x, norms[2])
    u, g = jnp.split(h @ w_up, 2, axis=-1)
    ext = jnp.concatenate([conv0.astype(u.dtype), u], axis=1)
    uc = conv_b
    for j in range(CONV_W):
        uc = uc + conv_w[j] * ext[:, j:j + T]
    ff = (jax.nn.gelu(uc) * g) @ w_down
    x = x + rmsnorm(ff, norms[3])
    conv_new = ext[:, T:]
    return x, k, v, c_new, n_new, m_new, s_new, conv_new


def setup_inputs(seed: int = 0) -> dict:
    key = jax.random.key(seed)
    ks = jax.random.split(key, 24)
    f32 = jnp.float32
    n_pages = PAST_LEN // PAGE_SIZE
    n_used = DEC_BATCH * n_pages
    n_phys = n_used + max(1, n_used // 4)

    def nrm(k, shape, s=1.0):
        return s * jax.random.normal(k, shape, f32)

    x_prompt = nrm(ks[0], (BATCH, SEQ, D_MODEL))
    x_sample = nrm(ks[1], (DEC_BATCH, DEC_SEQ, D_MODEL))
    cache_k = nrm(ks[2], (n_phys, DEPTH, PAGE_SIZE, H_MOBA, HEAD_DIM))
    cache_v = nrm(ks[3], (n_phys, DEPTH, PAGE_SIZE, H_MOBA, HEAD_DIM))
    page_table = jax.random.permutation(ks[4], n_phys)[:n_used].reshape(DEC_BATCH, n_pages).astype(jnp.int32)
    state_mlstm_c = nrm(ks[5], (DEPTH, DEC_BATCH, H_MLSTM, HEAD_DIM, HEAD_DIM), 0.3)
    state_mlstm_n = nrm(ks[6], (DEPTH, DEC_BATCH, H_MLSTM, HEAD_DIM), 0.3)
    state_mlstm_m = jax.random.uniform(ks[7], (DEPTH, DEC_BATCH, H_MLSTM), f32, 0.0, 3.0)
    state_hgrn = nrm(ks[8], (DEPTH, DEC_BATCH, H_HGRN, HEAD_DIM, HEAD_DIM), 0.5)
    state_conv = nrm(ks[9], (DEPTH, DEC_BATCH, CONV_W - 1, D_FF))
    w_in = nrm(ks[10], (DEPTH, D_MODEL, IN_W), D_MODEL ** -0.5)
    f_bias = jnp.linspace(3.0, 6.0, H_MLSTM, dtype=f32)
    mlstm_gate_bias = jnp.stack([nrm(ks[11], (DEPTH, H_MLSTM), 0.1),
                                 f_bias + nrm(ks[12], (DEPTH, H_MLSTM), 0.1)], axis=1)
    mlstm_norm = 1.0 + nrm(ks[13], (DEPTH, W_MLSTM), 0.05)
    hgrn_lb_logits = nrm(ks[14], (DEPTH, W_HGRN), 0.5)
    hgrn_norm = 1.0 + nrm(ks[15], (DEPTH, W_HGRN), 0.05)
    w_out = nrm(ks[16], (DEPTH, MIX_W, D_MODEL), MIX_W ** -0.5)
    norms = 1.0 + nrm(ks[17], (DEPTH, 4, D_MODEL), 0.05)
    ffn_w_up = nrm(ks[18], (DEPTH, D_MODEL, 2 * D_FF), D_MODEL ** -0.5)
    ffn_conv_w = nrm(ks[19], (DEPTH, CONV_W, D_FF), CONV_W ** -0.5)
    ffn_conv_b = nrm(ks[20], (DEPTH, D_FF), 0.01)
    ffn_w_down = nrm(ks[21], (DEPTH, D_FF, D_MODEL), D_FF ** -0.5)
    return {"x_prompt": x_prompt, "x_sample": x_sample, "cache_k": cache_k, "cache_v": cache_v,
            "page_table": page_table, "state_mlstm_c": state_mlstm_c, "state_mlstm_n": state_mlstm_n,
            "state_mlstm_m": state_mlstm_m, "state_hgrn": state_hgrn, "state_conv": state_conv,
            "w_in": w_in, "mlstm_gate_bias": mlstm_gate_bias, "mlstm_norm": mlstm_norm,
            "hgrn_lb_logits": hgrn_lb_logits, "hgrn_norm": hgrn_norm, "w_out": w_out, "norms": norms,
            "ffn_w_up": ffn_w_up, "ffn_conv_w": ffn_conv_w, "ffn_conv_b": ffn_conv_b, "ffn_w_down": ffn_w_down}


def reference(x_prompt, x_sample, cache_k, cache_v, page_table, state_mlstm_c, state_mlstm_n,
              state_mlstm_m, state_hgrn, state_conv, w_in, mlstm_gate_bias, mlstm_norm,
              hgrn_lb_logits, hgrn_norm, w_out, norms, ffn_w_up, ffn_conv_w, ffn_conv_b, ffn_w_down):
    f32 = jnp.float32
    Bp, Tp, _ = x_prompt.shape
    Bs = x_sample.shape[0]
    past_len = page_table.shape[1] * PAGE_SIZE
    lb_cs = jnp.cumsum(jax.nn.softmax(hgrn_lb_logits.astype(f32), axis=0), axis=0)
    lower_bounds = lb_cs - lb_cs[0:1]
    c0p = jnp.zeros((Bp, H_MLSTM, HEAD_DIM, HEAD_DIM), f32)
    n0p = jnp.zeros((Bp, H_MLSTM, HEAD_DIM), f32)
    m0p = jnp.zeros((Bp, H_MLSTM), f32)
    s0p = jnp.zeros((Bp, H_HGRN, HEAD_DIM, HEAD_DIM), f32)
    conv0p = jnp.zeros((Bp, CONV_W - 1, D_FF), x_prompt.dtype)
    yp, ys = x_prompt, x_sample
    rec_p, rec_s = [], []
    for l in range(DEPTH):
        lw = (w_in[l], mlstm_gate_bias[l], mlstm_norm[l], lower_bounds[l], hgrn_norm[l], w_out[l],
              norms[l], ffn_w_up[l], ffn_conv_w[l], ffn_conv_b[l], ffn_w_down[l])
        yp, *new_p = trunk_layer(yp, 0, None, None, c0p, n0p, m0p, s0p, conv0p, *lw)
        k_past = cache_k[page_table, l].reshape(Bs, past_len, H_MOBA, HEAD_DIM)
        v_past = cache_v[page_table, l].reshape(Bs, past_len, H_MOBA, HEAD_DIM)
        ys, *new_s = trunk_layer(ys, past_len, k_past, v_past, state_mlstm_c[l], state_mlstm_n[l],
                                 state_mlstm_m[l], state_hgrn[l], state_conv[l], *lw)
        rec_p.append(new_p)
        rec_s.append(new_s)

    def stack(rec, i, axis):
        return jnp.stack([r[i] for r in rec], axis=axis)

    page_shape = (Bp, DEPTH, Tp // PAGE_SIZE, PAGE_SIZE, H_MOBA, HEAD_DIM)
    k_prompt = stack(rec_p, 0, 1).reshape(page_shape).transpose(0, 2, 1, 3, 4, 5)
    v_prompt = stack(rec_p, 1, 1).reshape(page_shape).transpose(0, 2, 1, 3, 4, 5)
    mlstm_c_prompt = stack(rec_p, 2, 0)
    mlstm_n_prompt = stack(rec_p, 3, 0)
    mlstm_m_prompt = stack(rec_p, 4, 0)
    hgrn_prompt = stack(rec_p, 5, 0)
    conv_prompt = stack(rec_p, 6, 0)
    k_sample = stack(rec_s, 0, 1)
    v_sample = stack(rec_s, 1, 1)
    mlstm_c_sample = stack(rec_s, 2, 0)
    mlstm_n_sample = stack(rec_s, 3, 0)
    mlstm_m_sample = stack(rec_s, 4, 0)
    hgrn_sample = stack(rec_s, 5, 0)
    conv_sample = stack(rec_s, 6, 0)
    return (yp, ys, k_prompt, v_prompt, mlstm_c_prompt, mlstm_n_prompt, mlstm_m_prompt, hgrn_prompt,
            conv_prompt, k_sample, v_sample, mlstm_c_sample, mlstm_n_sample, mlstm_m_sample, hgrn_sample,
            conv_sample)
```

```python
import functools
import math

import jax
import jax.numpy as jnp
from jax import lax
from jax.experimental import pallas as pl
from jax.experimental.pallas import tpu as pltpu

F32 = jnp.float32
BF16 = jnp.bfloat16

D_MODEL = 1024
HEAD_DIM = 64
H_MLSTM = 4
H_MOBA = 8
H_HGRN = 4
W_MLSTM = H_MLSTM * HEAD_DIM
W_MOBA = H_MOBA * HEAD_DIM
W_HGRN = H_HGRN * HEAD_DIM
MOBA_BLOCK = 256
MOBA_TOPK = 3
ROPE_THETA = 10000.0
D_FF = 2816
CONV_W = 3
EPS = 1e-6
PAGE_SIZE = 128
GATE_W = 128
PROJ_W = 4 * W_MLSTM + 3 * W_MOBA + 4 * W_HGRN + GATE_W
NEG_BIG = -1e30
VMEM_LIMIT = 56 * 1024 * 1024


def _cparams(sem):
    return pltpu.CompilerParams(dimension_semantics=sem, vmem_limit_bytes=VMEM_LIMIT)


def _log_sigmoid(x):
    return jnp.minimum(x, 0.0) - jnp.log1p(jnp.exp(-jnp.abs(x)))


def _sigmoid(x):
    return 1.0 / (1.0 + jnp.exp(-x))


def _split3(x):
    h1 = x.astype(BF16)
    r1 = x - h1.astype(F32)
    h2 = r1.astype(BF16)
    h3 = (r1 - h2.astype(F32)).astype(BF16)
    return h1, h2, h3


def _cumsum_rows(tri, x):
    return sum(jnp.dot(tri, p, preferred_element_type=F32) for p in _split3(x))


def _cumsum_lanes(x, tri_u):
    return sum(jnp.dot(p, tri_u, preferred_element_type=F32) for p in _split3(x))


def _dot_nt(a, b):
    return lax.dot_general(a, b, (((1,), (1,)), ((), ())), preferred_element_type=F32)


def _dot_tn(a, b):
    return lax.dot_general(a, b, (((0,), (0,)), ((), ())), preferred_element_type=F32)


def _inproj_kernel(x_ref, g_ref, w_ref, cos_ref, sin_ref,
                   pm_ref, q_ref, k_ref, v_ref, ph_ref, gt_ref):
    x = x_ref[...]
    h = x * lax.rsqrt(jnp.mean(x * x, axis=-1, keepdims=True) + EPS) * g_ref[...]
    p = jnp.dot(h.astype(BF16), w_ref[...], preferred_element_type=F32)
    o = 4 * W_MLSTM
    pm_ref[...] = p[:, 0:o]
    cos = cos_ref[...]
    sin = sin_ref[...]
    lane = lax.broadcasted_iota(jnp.int32, cos.shape, 1)
    first_half = (lane % HEAD_DIM) < (HEAD_DIM // 2)

    def rope(a):
        swapped = jnp.where(first_half,
                            pltpu.roll(a, W_MOBA - HEAD_DIM // 2, axis=1),
                            pltpu.roll(a, HEAD_DIM // 2, axis=1))
        return a * cos + swapped * sin

    q_ref[...] = rope(p[:, o:o + W_MOBA])
    k_ref[...] = rope(p[:, o + W_MOBA:o + 2 * W_MOBA])
    v_ref[...] = p[:, o + 2 * W_MOBA:o + 3 * W_MOBA]
    o2 = o + 3 * W_MOBA
    ph_ref[...] = p[:, o2:o2 + 4 * W_HGRN]
    gt_ref[...] = p[:, o2 + 4 * W_HGRN:]


def _inproj(x2d, g, w, cos, sin, tm):
    n = x2d.shape[0]
    tab_blocks = cos.shape[0] // tm
    row = lambda i: (i, 0)
    const = lambda i: (0, 0)
    tab = lambda i: (i % tab_blocks, 0)
    widths = (4 * W_MLSTM, W_MOBA, W_MOBA, W_MOBA, 4 * W_HGRN, GATE_W)
    return pl.pallas_call(
        _inproj_kernel,
        grid=(n // tm,),
        in_specs=[pl.BlockSpec((tm, D_MODEL), row),
                  pl.BlockSpec((1, D_MODEL), const),
                  pl.BlockSpec((D_MODEL, PROJ_W), const),
                  pl.BlockSpec((tm, W_MOBA), tab),
                  pl.BlockSpec((tm, W_MOBA), tab)],
        out_specs=[pl.BlockSpec((tm, wd), row) for wd in widths],
        out_shape=[jax.ShapeDtypeStruct((n, wd), F32) for wd in widths],
        compiler_params=_cparams(("arbitrary",)),
        name="inproj",
    )(x2d, g, w, cos, sin)


def _mlstm_kernel(pm_ref, gc_ref, gr_ref, brow_ref, bcol_ref, norm_ref, c0_ref, n0_ref, m0_ref,
                  y_ref, c_ref, n_ref, m_ref, c_s, n_s, m_s, *, L):
    t = pl.program_id(1)

    @pl.when(t == 0)
    def _():
        c_s[...] = c0_ref[0]
        n_s[...] = n0_ref[0]
        m_s[...] = m0_ref[0]

    r = lax.broadcasted_iota(jnp.int32, (L, L), 0)
    c = lax.broadcasted_iota(jnp.int32, (L, L), 1)
    causal = r >= c
    tri_l = jnp.where(causal, 1.0, 0.0).astype(BF16)
    tri_u = jnp.where(r <= c, 1.0, 0.0).astype(BF16)

    pre_c = gc_ref[0] + brow_ref[...]
    b_c = _cumsum_rows(tri_l, _log_sigmoid(pre_c))
    pre_r = gr_ref[0] + bcol_ref[...]
    b_r = _cumsum_lanes(_log_sigmoid(pre_r), tri_u)

    scale = HEAD_DIM ** -0.5
    for h in range(H_MLSTM):
        sl = slice(h * HEAD_DIM, (h + 1) * HEAD_DIM)
        q = pm_ref[0, :, sl]
        k = pm_ref[0, :, W_MLSTM + h * HEAD_DIM:W_MLSTM + (h + 1) * HEAD_DIM] * scale
        v = pm_ref[0, :, 2 * W_MLSTM + h * HEAD_DIM:2 * W_MLSTM + (h + 1) * HEAD_DIM]
        og = pm_ref[0, :, 3 * W_MLSTM + h * HEAD_DIM:3 * W_MLSTM + (h + 1) * HEAD_DIM]
        i_col = pre_c[:, h:h + 1]
        b_col = b_c[:, H_MLSTM + h:H_MLSTM + h + 1]
        i_row = pre_r[h:h + 1, :]
        b_row = b_r[H_MLSTM + h:H_MLSTM + h + 1, :]
        m_prev = m_s[:, h:h + 1]
        cmat = c_s[h]
        n_row = n_s[h:h + 1, :]

        log_d = jnp.where(causal, b_col - b_row + i_row, -jnp.inf)
        m_inter = b_col + m_prev
        m_t = jnp.maximum(m_inter, jnp.max(log_d, axis=-1, keepdims=True))
        qb = q.astype(BF16)
        kb = k.astype(BF16)
        vb = v.astype(BF16)
        s = _dot_nt(qb, kb) * jnp.exp(log_d - m_t)
        g = jnp.exp(m_inter - m_t)
        num = g * jnp.dot(qb, cmat.astype(BF16), preferred_element_type=F32) \
            + jnp.dot(s.astype(BF16), vb, preferred_element_type=F32)
        den = g * jnp.sum(q * n_row, axis=-1, keepdims=True) + jnp.sum(s, axis=-1, keepdims=True)
        hh = num / jnp.maximum(jnp.abs(den), jnp.exp(-m_t))

        z = hh * _sigmoid(og)
        y = z * lax.rsqrt(jnp.mean(z * z, axis=-1, keepdims=True) + EPS) * norm_ref[:, sl]
        y_ref[0, :, sl] = y.astype(y_ref.dtype)

        b_last = b_row[:, L - 1:L]
        m_new = jnp.maximum(b_last + m_prev,
                            jnp.max(b_last - b_row + i_row, axis=-1, keepdims=True))
        decay = jnp.exp(b_last + m_prev - m_new)
        kw = k * jnp.exp(b_last - b_col + i_col - m_new)
        c_s[h] = decay * cmat + _dot_tn(kw.astype(BF16), vb)
        n_s[h:h + 1, :] = decay * n_row + jnp.sum(kw, axis=0, keepdims=True)
        m_s[:, h:h + 1] = m_new

    @pl.when(t == pl.num_programs(1) - 1)
    def _():
        c_ref[0] = c_s[...]
        n_ref[0] = n_s[...]
        m_ref[0] = m_s[...]


def _mlstm(pm, gates, gates_t, bias_row, bias_col, norm, c0, n0, m0, L):
    B, T, _ = pm.shape
    tok = lambda b, t: (b, t, 0)
    per_b3 = lambda b, t: (b, 0, 0)
    per_b4 = lambda b, t: (b, 0, 0, 0)
    const = lambda b, t: (0, 0)
    return pl.pallas_call(
        functools.partial(_mlstm_kernel, L=L),
        grid=(B, T // L),
        in_specs=[pl.BlockSpec((1, L, 4 * W_MLSTM), tok),
                  pl.BlockSpec((1, L, GATE_W), tok),
                  pl.BlockSpec((1, 8, L), lambda b, t: (b, 0, t)),
                  pl.BlockSpec((1, GATE_W), const),
                  pl.BlockSpec((8, 1), const),
                  pl.BlockSpec((1, W_MLSTM), const),
                  pl.BlockSpec((1, H_MLSTM, HEAD_DIM, HEAD_DIM), per_b4),
                  pl.BlockSpec((1, H_MLSTM, HEAD_DIM), per_b3),
                  pl.BlockSpec((1, 1, GATE_W), per_b3)],
        out_specs=[pl.BlockSpec((1, L, W_MLSTM), tok),
                   pl.BlockSpec((1, H_MLSTM, HEAD_DIM, HEAD_DIM), per_b4),
                   pl.BlockSpec((1, H_MLSTM, HEAD_DIM), per_b3),
                   pl.BlockSpec((1, 1, GATE_W), per_b3)],
        out_shape=[jax.ShapeDtypeStruct((B, T, W_MLSTM), BF16),
                   jax.ShapeDtypeStruct((B, H_MLSTM, HEAD_DIM, HEAD_DIM), F32),
                   jax.ShapeDtypeStruct((B, H_MLSTM, HEAD_DIM), F32),
                   jax.ShapeDtypeStruct((B, 1, GATE_W), F32)],
        scratch_shapes=[pltpu.VMEM((H_MLSTM, HEAD_DIM, HEAD_DIM), F32),
                        pltpu.VMEM((H_MLSTM, HEAD_DIM), F32),
                        pltpu.VMEM((1, GATE_W), F32)],
        compiler_params=_cparams(("arbitrary", "arbitrary")),
        name="mlstm",
    )(pm, gates, gates_t, bias_row, bias_col, norm, c0, n0, m0)


def _hgrn_kernel(ph_ref, llb_ref, l1m_ref, norm_ref, s0_ref, y_ref, s_ref, st_s, *, L, SC):
    t = pl.program_id(1)

    @pl.when(t == 0)
    def _():
        st_s[...] = s0_ref[0]

    r = lax.broadcasted_iota(jnp.int32, (L, L), 0)
    c = lax.broadcasted_iota(jnp.int32, (L, L), 1)
    tri_l = jnp.where(r >= c, 1.0, 0.0).astype(BF16)
    row_id = lax.broadcasted_iota(jnp.int32, (L, 1), 0)
    t3 = lax.broadcasted_iota(jnp.int32, (SC, SC, HEAD_DIM), 0)
    s3 = lax.broadcasted_iota(jnp.int32, (SC, SC, HEAD_DIM), 1)
    causal3 = t3 >= s3
    ones_k = jnp.ones((HEAD_DIM, HEAD_DIM), BF16)

    for h in range(H_HGRN):
        sl = slice(h * HEAD_DIM, (h + 1) * HEAD_DIM)
        q = ph_ref[0, :, sl]
        xf = ph_ref[0, :, W_HGRN + h * HEAD_DIM:W_HGRN + (h + 1) * HEAD_DIM]
        vi = ph_ref[0, :, 2 * W_HGRN + h * HEAD_DIM:2 * W_HGRN + (h + 1) * HEAD_DIM]
        gg = ph_ref[0, :, 3 * W_HGRN + h * HEAD_DIM:3 * W_HGRN + (h + 1) * HEAD_DIM]
        st = st_s[h]

        a = llb_ref[:, sl]
        cc = l1m_ref[:, sl] + _log_sigmoid(xf)
        mx = jnp.maximum(a, cc)
        lf = mx + jnp.log(jnp.exp(a - mx) + jnp.exp(cc - mx))
        kk = jnp.exp(l1m_ref[:, sl]) / (1.0 + jnp.exp(xf))
        b = _cumsum_rows(tri_l, lf)
        vib = vi.astype(BF16)

        o = _dot_nt((q * jnp.exp(b)).astype(BF16), st.astype(BF16))
        parts = []
        for i in range(L // SC):
            lo = i * SC
            b_i = b[lo:lo + SC]
            q_i = q[lo:lo + SC]
            d = jnp.exp(jnp.minimum(b_i[:, None, :] - b_i[None, :, :], 0.0))
            y3 = jnp.where(causal3, q_i[:, None, :] * d * kk[lo:lo + SC][None, :, :], 0.0)
            y2 = y3.reshape(SC * SC, HEAD_DIM)
            y_hi = y2.astype(BF16)
            y_lo = (y2 - y_hi.astype(F32)).astype(BF16)
            rs = (jnp.dot(y_hi, ones_k, preferred_element_type=F32)
                  + jnp.dot(y_lo, ones_k, preferred_element_type=F32))
            part = jnp.sum(rs.reshape(SC, SC, HEAD_DIM) * vi[lo:lo + SC][None, :, :], axis=1)
            if i > 0:
                b_ref = b[lo - 1:lo]
                q_t = q_i * jnp.exp(b_i - b_ref)
                k_t = jnp.where(row_id < lo, kk * jnp.exp(jnp.minimum(b_ref - b, 0.0)), 0.0)
                a_off = _dot_nt(q_t.astype(BF16), k_t.astype(BF16))
                part = part + jnp.dot(a_off.astype(BF16), vib, preferred_element_type=F32)
            parts.append(part)
        o = o + jnp.concatenate(parts, axis=0)

        y = o * lax.rsqrt(jnp.mean(o * o, axis=-1, keepdims=True) + EPS) * norm_ref[:, sl]
        y = y * (gg * _sigmoid(gg))
        y_ref[0, :, sl] = y.astype(y_ref.dtype)

        b_last = b[L - 1:L]
        kd = kk * jnp.exp(b_last - b)
        st_s[h] = st * jnp.exp(b_last) + _dot_tn(vib, kd.astype(BF16))

    @pl.when(t == pl.num_programs(1) - 1)
    def _():
        s_ref[0] = st_s[...]


def _hgrn(ph, log_lb, log1m_lb, norm, s0t, L):
    B, T, _ = ph.shape
    SC = min(16, L)
    tok = lambda b, t: (b, t, 0)
    per_b4 = lambda b, t: (b, 0, 0, 0)
    const = lambda b, t: (0, 0)
    return pl.pallas_call(
        functools.partial(_hgrn_kernel, L=L, SC=SC),
        grid=(B, T // L),
        in_specs=[pl.BlockSpec((1, L, 4 * W_HGRN), tok),
                  pl.BlockSpec((1, W_HGRN), const),
                  pl.BlockSpec((1, W_HGRN), const),
                  pl.BlockSpec((1, W_HGRN), const),
                  pl.BlockSpec((1, H_HGRN, HEAD_DIM, HEAD_DIM), per_b4)],
        out_specs=[pl.BlockSpec((1, L, W_HGRN), tok),
                   pl.BlockSpec((1, H_HGRN, HEAD_DIM, HEAD_DIM), per_b4)],
        out_shape=[jax.ShapeDtypeStruct((B, T, W_HGRN), BF16),
                   jax.ShapeDtypeStruct((B, H_HGRN, HEAD_DIM, HEAD_DIM), F32)],
        scratch_shapes=[pltpu.VMEM((H_HGRN, HEAD_DIM, HEAD_DIM), F32)],
        compiler_params=_cparams(("arbitrary", "arbitrary")),
        name="hgrn2",
    )(ph, log_lb, log1m_lb, norm, s0t)


def _topk_mask(gate, n_valid, lane, n_max):
    cnt = jnp.zeros(gate.shape, F32)
    for n in range(n_max):
        col = gate[:, n:n + 1]
        ahead = jnp.where(lane > n, jnp.where(col >= gate, 1.0, 0.0), jnp.where(col > gate, 1.0, 0.0))
        cnt = cnt + ahead * jnp.where(n < n_valid, 1.0, 0.0)
    return jnp.where(cnt < MOBA_TOPK, jnp.where(lane < n_valid, 1.0, 0.0), 0.0)


def _moba_prompt_kernel(q_ref, k_ref, v_ref, y_ref, km_s, *, NB):
    j = pl.program_id(2)
    BS = MOBA_BLOCK

    @pl.when(j == 0)
    def _():
        km_s[...] = jnp.zeros(km_s.shape, F32)
        for n in range(NB):
            km_s[n:n + 1, :] = jnp.mean(k_ref[0, n * BS:(n + 1) * BS, :], axis=0, keepdims=True)

    lane = lax.broadcasted_iota(jnp.int32, (BS, 128), 1)
    qi = lax.broadcasted_iota(jnp.int32, (BS, BS), 0)
    ki = lax.broadcasted_iota(jnp.int32, (BS, BS), 1)
    own0 = pl.multiple_of(j * BS, BS)

    for h in range(2):
        sl = slice(h * HEAD_DIM, (h + 1) * HEAD_DIM)
        q = q_ref[0, :, sl]
        gate = lax.dot_general(q, km_s[:, sl], (((1,), (1,)), ((), ())),
                               precision=lax.Precision.HIGHEST, preferred_element_type=F32)
        sel = _topk_mask(gate, j, lane, NB)
        qb = (q * (HEAD_DIM ** -0.5)).astype(BF16)

        def attend(carry, kb, vb, keep):
            m, l, acc = carry
            s = jnp.where(keep, _dot_nt(qb, kb), NEG_BIG)
            m_new = jnp.maximum(m, jnp.max(s, axis=-1, keepdims=True))
            alpha = jnp.exp(m - m_new)
            p = jnp.exp(s - m_new)
            l = alpha * l + jnp.sum(p, axis=-1, keepdims=True)
            acc = alpha * acc + jnp.dot(p.astype(BF16), vb, preferred_element_type=F32)
            return m_new, l, acc

        def past_block(n, carry):
            off = pl.multiple_of(n * BS, BS)
            kb = k_ref[0, pl.ds(off, BS), sl].astype(BF16)
            vb = v_ref[0, pl.ds(off, BS), sl].astype(BF16)
            chosen = jnp.sum(jnp.where(lane == n, sel, 0.0), axis=-1, keepdims=True) > 0.0
            return attend(carry, kb, vb, chosen)

        init = (jnp.full((BS, 1), NEG_BIG, F32), jnp.zeros((BS, 1), F32),
                jnp.zeros((BS, HEAD_DIM), F32))
        carry = lax.fori_loop(0, j, past_block, init)
        kb = k_ref[0, pl.ds(own0, BS), sl].astype(BF16)
        vb = v_ref[0, pl.ds(own0, BS), sl].astype(BF16)
        m, l, acc = attend(carry, kb, vb, ki <= qi)
        y_ref[0, :, sl] = (acc / l).astype(y_ref.dtype)


def _moba_prompt(q, k, v):
    B, T, _ = q.shape
    NB = T // MOBA_BLOCK
    HP = 2 * HEAD_DIM
    return pl.pallas_call(
        functools.partial(_moba_prompt_kernel, NB=NB),
        grid=(B, W_MOBA // HP, NB),
        in_specs=[pl.BlockSpec((1, MOBA_BLOCK, HP), lambda b, h, j: (b, j, h)),
                  pl.BlockSpec((1, T, HP), lambda b, h, j: (b, 0, h)),
                  pl.BlockSpec((1, T, HP), lambda b, h, j: (b, 0, h))],
        out_specs=pl.BlockSpec((1, MOBA_BLOCK, HP), lambda b, h, j: (b, j, h)),
        out_shape=jax.ShapeDtypeStruct((B, T, W_MOBA), BF16),
        scratch_shapes=[pltpu.VMEM((128, HP), F32)],
        compiler_params=_cparams(("arbitrary", "arbitrary", "arbitrary")),
        name="moba_prompt",
    )(q, k, v)


Q_ROWS = 8


def _moba_sample_kernel(pt_ref, q_ref, kn_ref, vn_ref, ck_hbm, cv_hbm, y_ref,
                        kbuf, vbuf, sem, gate_s, m_s, l_s, o_s, *, layer, PPS, n_new):
    b = pl.program_id(0)
    c = pl.program_id(1)
    n_chunks = pl.num_programs(1)
    step = b * n_chunks + c
    total = pl.num_programs(0) * n_chunks
    slot = step % 2
    nb_per = PPS // 2
    BS = MOBA_BLOCK
    R = H_MOBA * Q_ROWS

    def page_copies(bb, cc, sl):
        out = []
        for p in range(PPS):
            page = pt_ref[bb, cc * PPS + p]
            out.append(pltpu.make_async_copy(ck_hbm.at[page, layer], kbuf.at[sl, p], sem.at[0, sl]))
            out.append(pltpu.make_async_copy(cv_hbm.at[page, layer], vbuf.at[sl, p], sem.at[1, sl]))
        return out

    @pl.when(step == 0)
    def _():
        for cp in page_copies(b, c, slot):
            cp.start()

    @pl.when(step + 1 < total)
    def _():
        nxt = step + 1
        for cp in page_copies(nxt // n_chunks, nxt % n_chunks, 1 - slot):
            cp.start()

    for cp in page_copies(b, c, slot):
        cp.wait()

    @pl.when(c == 0)
    def _():
        gate_s[...] = jnp.zeros(gate_s.shape, F32)
        m_s[...] = jnp.zeros(m_s.shape, F32)
        l_s[...] = jnp.zeros(l_s.shape, F32)

    rows = lax.broadcasted_iota(jnp.int32, (R, W_MOBA), 0)
    lanes = lax.broadcasted_iota(jnp.int32, (R, W_MOBA), 1)
    diag = (rows // Q_ROWS) == (lanes // HEAD_DIM)
    q8 = q_ref[0]
    qbd = jnp.where(diag, jnp.broadcast_to(q8[None], (H_MOBA, Q_ROWS, W_MOBA)).reshape(R, W_MOBA), 0.0)
    qbd_s = qbd * (HEAD_DIM ** -0.5)
    qbd_b = qbd_s.astype(BF16)
    lane = lax.broadcasted_iota(jnp.int32, (R, 128), 1)

    for i in range(nb_per):
        kblk = kbuf[slot, 2 * i:2 * i + 2].reshape(BS, W_MOBA)
        vblk = vbuf[slot, 2 * i:2 * i + 2].reshape(BS, W_MOBA)
        n = c * nb_per + i
        kmean = jnp.mean(kblk, axis=0, keepdims=True)
        gcol = jnp.sum(qbd * kmean, axis=-1, keepdims=True)
        s = _dot_nt(qbd_b, kblk.astype(BF16))
        mcol = jnp.max(s, axis=-1, keepdims=True)
        p = jnp.exp(s - mcol)
        lcol = jnp.sum(p, axis=-1, keepdims=True)
        o = jnp.dot(p.astype(BF16), vblk.astype(BF16), preferred_element_type=F32)
        o_s[n] = jnp.where(diag, o, 0.0)
        hit = lane == n
        gate_s[...] = jnp.where(hit, gcol, gate_s[...])
        m_s[...] = jnp.where(hit, mcol, m_s[...])
        l_s[...] = jnp.where(hit, lcol, l_s[...])

    @pl.when(c == n_chunks - 1)
    def _():
        n_blk = o_s.shape[0]
        sel = _topk_mask(gate_s[...], n_blk, lane, n_blk) > 0.0
        row_t = lax.broadcasted_iota(jnp.int32, (R, 1), 0) % Q_ROWS
        own = []
        for t in range(n_new):
            st = jnp.sum(qbd_s * kn_ref[0, t:t + 1, :], axis=-1, keepdims=True)
            own.append(jnp.where(row_t >= t, st, NEG_BIG))
        m_tot = jnp.max(jnp.where(sel, m_s[...], NEG_BIG), axis=-1, keepdims=True)
        for st in own:
            m_tot = jnp.maximum(m_tot, st)
        w = jnp.where(sel, jnp.exp(m_s[...] - m_tot), 0.0)
        den = jnp.sum(w * l_s[...], axis=-1, keepdims=True)
        acc = jnp.zeros((R, W_MOBA), F32)
        for n in range(n_blk):
            acc = acc + w[:, n:n + 1] * o_s[n]
        for t, st in enumerate(own):
            pt = jnp.exp(st - m_tot)
            den = den + pt
            acc = acc + pt * jnp.where(diag, vn_ref[0, t:t + 1, :], 0.0)
        acc = acc / den
        y = acc[0:Q_ROWS]
        for h in range(1, H_MOBA):
            y = y + acc[h * Q_ROWS:(h + 1) * Q_ROWS]
        y_ref[0] = y.astype(y_ref.dtype)


def _moba_sample(q8, kn8, vn8, cache_k, cache_v, page_table, layer, n_new):
    B = q8.shape[0]
    n_pages = page_table.shape[1]
    PPS = min(16, n_pages)
    n_chunks = n_pages // PPS
    n_blk = n_pages * PAGE_SIZE // MOBA_BLOCK
    R = H_MOBA * Q_ROWS
    per_b = lambda b, c, pt: (b, 0, 0)
    grid_spec = pltpu.PrefetchScalarGridSpec(
        num_scalar_prefetch=1,
        grid=(B, n_chunks),
        in_specs=[pl.BlockSpec((1, Q_ROWS, W_MOBA), per_b),
                  pl.BlockSpec((1, Q_ROWS, W_MOBA), per_b),
                  pl.BlockSpec((1, Q_ROWS, W_MOBA), per_b),
                  pl.BlockSpec(memory_space=pl.ANY),
                  pl.BlockSpec(memory_space=pl.ANY)],
        out_specs=pl.BlockSpec((1, Q_ROWS, W_MOBA), per_b),
        scratch_shapes=[pltpu.VMEM((2, PPS, PAGE_SIZE, W_MOBA), F32),
                        pltpu.VMEM((2, PPS, PAGE_SIZE, W_MOBA), F32),
                        pltpu.SemaphoreType.DMA((2, 2)),
                        pltpu.VMEM((R, 128), F32),
                        pltpu.VMEM((R, 128), F32),
                        pltpu.VMEM((R, 128), F32),
                        pltpu.VMEM((n_blk, R, W_MOBA), F32)])
    return pl.pallas_call(
        functools.partial(_moba_sample_kernel, layer=layer, PPS=PPS, n_new=n_new),
        grid_spec=grid_spec,
        out_shape=jax.ShapeDtypeStruct((B, Q_ROWS, W_MOBA), BF16),
        compiler_params=_cparams(("arbitrary", "arbitrary")),
        name="moba_sample",
    )(page_table, q8, kn8, vn8, cache_k, cache_v)


def _outproj_kernel(x_ref, ya_ref, yb_ref, yc_ref, w_ref, g_ref, o_ref):
    a0, a1 = W_MLSTM, W_MLSTM + W_MOBA
    mix = (jnp.dot(ya_ref[...], w_ref[0:a0, :], preferred_element_type=F32)
           + jnp.dot(yb_ref[...], w_ref[a0:a1, :], preferred_element_type=F32)
           + jnp.dot(yc_ref[...], w_ref[a1:, :], preferred_element_type=F32))
    o_ref[...] = x_ref[...] + mix * lax.rsqrt(jnp.mean(mix * mix, axis=-1, keepdims=True) + EPS) * g_ref[...]


def _outproj(x2d, ya, yb, yc, w, g, tm):
    n = x2d.shape[0]
    row = lambda i: (i, 0)
    const = lambda i: (0, 0)
    return pl.pallas_call(
        _outproj_kernel,
        grid=(n // tm,),
        in_specs=[pl.BlockSpec((tm, D_MODEL), row),
                  pl.BlockSpec((tm, W_MLSTM), row),
                  pl.BlockSpec((tm, W_MOBA), row),
                  pl.BlockSpec((tm, W_HGRN), row),
                  pl.BlockSpec((D_MODEL, D_MODEL), const),
                  pl.BlockSpec((1, D_MODEL), const)],
        out_specs=pl.BlockSpec((tm, D_MODEL), row),
        out_shape=jax.ShapeDtypeStruct((n, D_MODEL), F32),
        compiler_params=_cparams(("arbitrary",)),
        name="outproj",
    )(x2d, ya, yb, yc, w, g)


FF_CHUNKS = 2
FF_CHUNK = D_FF // FF_CHUNKS


def _gelu_tanh(x):
    return 0.5 * x * (1.0 + jnp.tanh(math.sqrt(2.0 / math.pi) * (x + 0.044715 * (x * x * x))))


def _ffn_kernel(x_ref, g2_ref, wu_ref, wg_ref, wd_ref, cw_ref, cb_ref, c0_ref, g3_ref,
                o_ref, tail_ref, hb_s, acc_s, ext_s, *, tm, R, S):
    tt = pl.program_id(1)
    c = pl.program_id(2)

    @pl.when(c == 0)
    def _():
        x = x_ref[...]
        h = x * lax.rsqrt(jnp.mean(x * x, axis=-1, keepdims=True) + EPS) * g2_ref[...]
        hb_s[...] = h.astype(BF16)

    @pl.when(tt == 0)
    def _():
        ext_s[c, 0:R, :] = c0_ref[0]

    hb = hb_s[...]
    u = jnp.dot(hb, wu_ref[...], preferred_element_type=F32)
    g = jnp.dot(hb, wg_ref[...], preferred_element_type=F32)
    ext_s[c, R:R + tm, :] = u
    uc = (cb_ref[...]
          + cw_ref[0:1, :] * ext_s[c, R - 2 * S:R - 2 * S + tm, :]
          + cw_ref[1:2, :] * ext_s[c, R - S:R - S + tm, :]
          + cw_ref[2:3, :] * u)
    act = (_gelu_tanh(uc) * g).astype(BF16)
    d = jnp.dot(act, wd_ref[...], preferred_element_type=F32)

    @pl.when(c == 0)
    def _():
        acc_s[...] = d

    @pl.when(c > 0)
    def _():
        acc_s[...] += d

    tail = ext_s[c, tm:tm + R, :]
    tail_ref[0, c] = tail
    ext_s[c, 0:R, :] = tail

    @pl.when(c == pl.num_programs(2) - 1)
    def _():
        a = acc_s[...]
        o_ref[...] = x_ref[...] + a * lax.rsqrt(jnp.mean(a * a, axis=-1, keepdims=True) + EPS) * g3_ref[...]


def _ffn(x2d, g2, w_up, w_down, conv_w8, conv_b, carry0, g3, *, nseq, tm, R, S):
    n = x2d.shape[0]
    ntt = n // (nseq * tm)
    row = lambda s, t, c: (s * ntt + t, 0)
    const = lambda s, t, c: (0, 0)
    x_new, tail = pl.pallas_call(
        functools.partial(_ffn_kernel, tm=tm, R=R, S=S),
        grid=(nseq, ntt, FF_CHUNKS),
        in_specs=[pl.BlockSpec((tm, D_MODEL), row),
                  pl.BlockSpec((1, D_MODEL), const),
                  pl.BlockSpec((D_MODEL, FF_CHUNK), lambda s, t, c: (0, c)),
                  pl.BlockSpec((D_MODEL, FF_CHUNK), lambda s, t, c: (0, FF_CHUNKS + c)),
                  pl.BlockSpec((FF_CHUNK, D_MODEL), lambda s, t, c: (c, 0)),
                  pl.BlockSpec((8, FF_CHUNK), lambda s, t, c: (0, c)),
                  pl.BlockSpec((1, FF_CHUNK), lambda s, t, c: (0, c)),
                  pl.BlockSpec((1, R, FF_CHUNK), lambda s, t, c: (s, 0, c)),
                  pl.BlockSpec((1, D_MODEL), const)],
        out_specs=[pl.BlockSpec((tm, D_MODEL), row),
                   pl.BlockSpec((1, FF_CHUNKS, R, FF_CHUNK), lambda s, t, c: (s, 0, 0, 0))],
        out_shape=[jax.ShapeDtypeStruct((n, D_MODEL), F32),
                   jax.ShapeDtypeStruct((nseq, FF_CHUNKS, R, FF_CHUNK), F32)],
        scratch_shapes=[pltpu.VMEM((tm, D_MODEL), BF16),
                        pltpu.VMEM((tm, D_MODEL), F32),
                        pltpu.VMEM((FF_CHUNKS, R + tm, FF_CHUNK), F32)],
        compiler_params=_cparams(("arbitrary", "arbitrary", "arbitrary")),
        name="convffn",
    )(x2d, g2, w_up, w_up, w_down, conv_w8, conv_b, carry0, g3)
    return x_new, jnp.transpose(tail, (0, 2, 1, 3)).reshape(nseq, R, D_FF)


def _rope_tables(pos):
    half = HEAD_DIM // 2
    inv = ROPE_THETA ** (-jnp.arange(half, dtype=F32) / half)
    ang = pos.astype(F32)[:, None] * inv[None, :]
    cos = jnp.cos(ang)
    sin = jnp.sin(ang)
    cos_t = jnp.tile(jnp.concatenate([cos, cos], axis=-1), (1, H_MOBA))
    sin_t = jnp.tile(jnp.concatenate([-sin, sin], axis=-1), (1, H_MOBA))
    return cos_t, sin_t


def _layer_weights(l, w_in_r, gate_bias, mlstm_norm, lower_bounds, hgrn_norm, w_out_b, norms,
                   w_up_b, conv_w, conv_b, w_down_b):
    bias = gate_bias[l].reshape(2 * H_MLSTM)
    lb = lower_bounds[l]
    return dict(
        w_in=w_in_r[l], w_out=w_out_b[l], w_up=w_up_b[l], w_down=w_down_b[l],
        g0=norms[l, 0][None], g1=norms[l, 1][None], g2=norms[l, 2][None], g3=norms[l, 3][None],
        bias_row=jnp.pad(bias, (0, GATE_W - 2 * H_MLSTM))[None],
        bias_col=bias[:, None],
        m_norm=mlstm_norm[l][None], h_norm=hgrn_norm[l][None],
        log_lb=jnp.log(lb)[None], log1m_lb=jnp.log1p(-lb)[None],
        conv_w8=jnp.pad(conv_w[l], ((0, 8 - CONV_W), (0, 0))), conv_b=conv_b[l][None])


def _layer_prompt(x2d, B, T, lw, cos, sin):
    pm, q, k, v, ph, gt = _inproj(x2d, lw["g0"], lw["w_in"], cos, sin, 256)
    r3 = lambda a: a.reshape(B, T, a.shape[-1])
    gates = r3(gt)
    gates_t = jnp.transpose(gates[:, :, :2 * H_MLSTM], (0, 2, 1))
    c0 = jnp.zeros((B, H_MLSTM, HEAD_DIM, HEAD_DIM), F32)
    n0 = jnp.zeros((B, H_MLSTM, HEAD_DIM), F32)
    m0 = jnp.zeros((B, 1, GATE_W), F32)
    ya, c_new, n_new, m_new = _mlstm(r3(pm), gates, gates_t, lw["bias_row"], lw["bias_col"],
                                     lw["m_norm"], c0, n0, m0, min(128, T))
    yb = _moba_prompt(r3(q), r3(k), r3(v))
    s0t = jnp.zeros((B, H_HGRN, HEAD_DIM, HEAD_DIM), F32)
    yc, st = _hgrn(r3(ph), lw["log_lb"], lw["log1m_lb"], lw["h_norm"], s0t, min(64, T))
    r2 = lambda a: a.reshape(B * T, a.shape[-1])
    x1 = _outproj(x2d, r2(ya), r2(yb), r2(yc), lw["w_out"], lw["g1"], 512)
    carry0 = jnp.zeros((B, 8, D_FF), F32)
    x2, tail = _ffn(x1, lw["g2"], lw["w_up"], lw["w_down"], lw["conv_w8"], lw["conv_b"], carry0,
                    lw["g3"], nseq=B, tm=min(512, T), R=8, S=1)
    return (x2, k, v, c_new, n_new, m_new[:, 0, :H_MLSTM], jnp.swapaxes(st, -1, -2),
            tail[:, 8 - (CONV_W - 1):])


SAMPLE_CHUNK = 8


def _layer_sample(x2d, B, T, lw, cos, sin, cache_k, cache_v, page_table, layer, c0, n0, m0, s0, conv0):
    pm, q, k, v, ph, gt = _inproj(x2d, lw["g0"], lw["w_in"], cos, sin, B * T)
    L = SAMPLE_CHUNK

    def seq_major(a, pad_row=None):
        a = jnp.transpose(a.reshape(T, B, a.shape[-1]), (1, 0, 2))
        if pad_row is None:
            return jnp.pad(a, ((0, 0), (0, L - T), (0, 0)))
        fill = jnp.broadcast_to(pad_row[None, None, :], (B, L - T, a.shape[-1]))
        return jnp.concatenate([a, fill], axis=1)

    def time_major(a):
        return jnp.transpose(a[:, :T], (1, 0, 2)).reshape(T * B, a.shape[-1])

    gate_pad = jnp.where(jnp.arange(GATE_W) < H_MLSTM, NEG_BIG, -NEG_BIG).astype(F32)
    gates = seq_major(gt, gate_pad)
    gates_t = jnp.transpose(gates[:, :, :2 * H_MLSTM], (0, 2, 1))
    m0p = jnp.pad(m0, ((0, 0), (0, GATE_W - H_MLSTM)))[:, None, :]
    ya, c_new, n_new, m_new = _mlstm(seq_major(pm), gates, gates_t, lw["bias_row"], lw["bias_col"],
                                     lw["m_norm"], c0, n0, m0p, L)
    col = jnp.arange(4 * W_HGRN)
    h_pad = jnp.where((col >= W_HGRN) & (col < 2 * W_HGRN), -NEG_BIG, 0.0).astype(F32)
    yc, st = _hgrn(seq_major(ph, h_pad), lw["log_lb"], lw["log1m_lb"], lw["h_norm"],
                   jnp.swapaxes(s0, -1, -2), L)
    yb = _moba_sample(seq_major(q), seq_major(k), seq_major(v), cache_k, cache_v, page_table, layer, T)
    x1 = _outproj(x2d, time_major(ya), time_major(yb), time_major(yc), lw["w_out"], lw["g1"], B * T)
    carry0 = jnp.transpose(conv0, (1, 0, 2)).reshape(1, (CONV_W - 1) * B, D_FF)
    x2, tail = _ffn(x1, lw["g2"], lw["w_up"], lw["w_down"], lw["conv_w8"], lw["conv_b"], carry0,
                    lw["g3"], nseq=1, tm=B * T, R=(CONV_W - 1) * B, S=B)
    conv_new = jnp.transpose(tail.reshape(CONV_W - 1, B, D_FF), (1, 0, 2))
    tb = lambda a: jnp.transpose(a.reshape(T, B, H_MOBA, HEAD_DIM), (1, 0, 2, 3))
    return (x2, tb(k), tb(v), c_new, n_new, m_new[:, 0, :H_MLSTM], jnp.swapaxes(st, -1, -2), conv_new)


def kernel(x_prompt, x_sample, cache_k, cache_v, page_table, state_mlstm_c, state_mlstm_n,
           state_mlstm_m, state_hgrn, state_conv, w_in, mlstm_gate_bias, mlstm_norm,
           hgrn_lb_logits, hgrn_norm, w_out, norms, ffn_w_up, ffn_conv_w, ffn_conv_b, ffn_w_down):
    Bp, Tp, _ = x_prompt.shape
    Bs, Ts, _ = x_sample.shape
    depth = w_in.shape[0]
    past_len = page_table.shape[1] * PAGE_SIZE

    lb_cs = jnp.cumsum(jax.nn.softmax(hgrn_lb_logits.astype(F32), axis=0), axis=0)
    lower_bounds = lb_cs - lb_cs[0:1]
    g0 = 4 * W_MLSTM
    g1 = g0 + 2 * H_MLSTM
    w_in_r = jnp.concatenate(
        [w_in[:, :, :g0], w_in[:, :, g1:], w_in[:, :, g0:g1],
         jnp.zeros((depth, D_MODEL, GATE_W - 2 * H_MLSTM), w_in.dtype)], axis=-1).astype(BF16)
    w_out_b = w_out.astype(BF16)
    w_up_b = ffn_w_up.astype(BF16)
    w_down_b = ffn_w_down.astype(BF16)
    cos_p, sin_p = _rope_tables(jnp.arange(Tp))
    cos_s, sin_s = _rope_tables(past_len + jnp.arange(Ts * Bs) // Bs)
    ck = cache_k.reshape(cache_k.shape[0], depth, PAGE_SIZE, W_MOBA)
    cv = cache_v.reshape(cache_v.shape[0], depth, PAGE_SIZE, W_MOBA)

    xp = x_prompt.reshape(Bp * Tp, D_MODEL)
    xs = jnp.transpose(x_sample, (1, 0, 2)).reshape(Ts * Bs, D_MODEL)
    rec_p, rec_s = [], []
    for l in range(depth):
        lw = _layer_weights(l, w_in_r, mlstm_gate_bias, mlstm_norm, lower_bounds, hgrn_norm,
                            w_out_b, norms, w_up_b, ffn_conv_w, ffn_conv_b, w_down_b)
        xp, *new_p = _layer_prompt(xp, Bp, Tp, lw, cos_p, sin_p)
        xs, *new_s = _layer_sample(xs, Bs, Ts, lw, cos_s, sin_s, ck, cv, page_table, l,
                                   state_mlstm_c[l], state_mlstm_n[l], state_mlstm_m[l],
                                   state_hgrn[l], state_conv[l])
        rec_p.append(new_p)
        rec_s.append(new_s)

    def stack(rec, i, axis):
        return jnp.stack([r[i] for r in rec], axis=axis)

    n_pg = Tp // PAGE_SIZE
    page = lambda a: a.reshape(Bp, n_pg, PAGE_SIZE, H_MOBA, HEAD_DIM)
    k_prompt = jnp.stack([page(r[0]) for r in rec_p], axis=2)
    v_prompt = jnp.stack([page(r[1]) for r in rec_p], axis=2)
    y_prompt = xp.reshape(Bp, Tp, D_MODEL)
    y_sample = jnp.transpose(xs.reshape(Ts, Bs, D_MODEL), (1, 0, 2))
    return (y_prompt, y_sample, k_prompt, v_prompt,
            stack(rec_p, 2, 0), stack(rec_p, 3, 0), stack(rec_p, 4, 0), stack(rec_p, 5, 0),
            stack(rec_p, 6, 0),
            stack(rec_s, 0, 1), stack(rec_s, 1, 1),
            stack(rec_s, 2, 0), stack(rec_s, 3, 0), stack(rec_s, 4, 0), stack(rec_s, 5, 0),
            stack(rec_s, 6, 0))
```

```python
import functools
import math

import jax
import numpy as np
import jax.numpy as jnp
from jax import lax
from jax.experimental import pallas as pl
from jax.experimental.pallas import tpu as pltpu

F32 = jnp.float32
BF16 = jnp.bfloat16

D_MODEL = 1024
HEAD_DIM = 64
H_MLSTM = 4
H_MOBA = 8
H_HGRN = 4
W_MLSTM = H_MLSTM * HEAD_DIM
W_MOBA = H_MOBA * HEAD_DIM
W_HGRN = H_HGRN * HEAD_DIM
MOBA_BLOCK = 256
MOBA_TOPK = 3
ROPE_THETA = 10000.0
D_FF = 2816
CONV_W = 3
EPS = 1e-6
PAGE_SIZE = 128
GATE_W = 128
PROJ_W = 4 * W_MLSTM + 3 * W_MOBA + 4 * W_HGRN + GATE_W
NEG_BIG = -1e30
VMEM_LIMIT = 56 * 1024 * 1024


def _cparams(sem):
    return pltpu.CompilerParams(dimension_semantics=sem, vmem_limit_bytes=VMEM_LIMIT)


def _log_sigmoid(x):
    return jnp.minimum(x, 0.0) - jnp.log1p(jnp.exp(-jnp.abs(x)))


def _sigmoid(x):
    return 1.0 / (1.0 + jnp.exp(-x))


def _split3(x):
    h1 = x.astype(BF16)
    r1 = x - h1.astype(F32)
    h2 = r1.astype(BF16)
    h3 = (r1 - h2.astype(F32)).astype(BF16)
    return h1, h2, h3


def _cumsum_rows(tri, x):
    return sum(jnp.dot(tri, p, preferred_element_type=F32) for p in _split3(x))


def _cumsum_lanes(x, tri_u):
    return sum(jnp.dot(p, tri_u, preferred_element_type=F32) for p in _split3(x))


def _dot_nt(a, b):
    return lax.dot_general(a, b, (((1,), (1,)), ((), ())), preferred_element_type=F32)


def _dot_tn(a, b):
    return lax.dot_general(a, b, (((0,), (0,)), ((), ())), preferred_element_type=F32)


def _inproj_kernel(x_ref, g_ref, w_ref, cos_ref, sin_ref,
                   pm_ref, q_ref, k_ref, v_ref, ph_ref, gt_ref):
    x = x_ref[...]
    h = x * lax.rsqrt(jnp.mean(x * x, axis=-1, keepdims=True) + EPS) * g_ref[...]
    p = jnp.dot(h.astype(BF16), w_ref[...], preferred_element_type=F32)
    o = 4 * W_MLSTM
    pm_ref[...] = p[:, 0:o]
    cos = cos_ref[...]
    sin = sin_ref[...]
    lane = lax.broadcasted_iota(jnp.int32, cos.shape, 1)
    first_half = (lane % HEAD_DIM) < (HEAD_DIM // 2)

    def rope(a):
        swapped = jnp.where(first_half,
                            pltpu.roll(a, W_MOBA - HEAD_DIM // 2, axis=1),
                            pltpu.roll(a, HEAD_DIM // 2, axis=1))
        return a * cos + swapped * sin

    q_ref[...] = rope(p[:, o:o + W_MOBA])
    k_ref[...] = rope(p[:, o + W_MOBA:o + 2 * W_MOBA])
    v_ref[...] = p[:, o + 2 * W_MOBA:o + 3 * W_MOBA]
    o2 = o + 3 * W_MOBA
    ph_ref[...] = p[:, o2:o2 + 4 * W_HGRN]
    gt_ref[...] = p[:, o2 + 4 * W_HGRN:]


def _inproj(x2d, g, w, cos, sin, tm):
    n = x2d.shape[0]
    tab_blocks = cos.shape[0] // tm
    row = lambda i: (i, 0)
    const = lambda i: (0, 0)
    tab = lambda i: (i % tab_blocks, 0)
    widths = (4 * W_MLSTM, W_MOBA, W_MOBA, W_MOBA, 4 * W_HGRN, GATE_W)
    return pl.pallas_call(
        _inproj_kernel,
        grid=(n // tm,),
        in_specs=[pl.BlockSpec((tm, D_MODEL), row),
                  pl.BlockSpec((1, D_MODEL), const),
                  pl.BlockSpec((D_MODEL, PROJ_W), const),
                  pl.BlockSpec((tm, W_MOBA), tab),
                  pl.BlockSpec((tm, W_MOBA), tab)],
        out_specs=[pl.BlockSpec((tm, wd), row) for wd in widths],
        out_shape=[jax.ShapeDtypeStruct((n, wd), F32) for wd in widths],
        compiler_params=_cparams(("arbitrary",)),
        name="inproj",
    )(x2d, g, w, cos, sin)


def _mlstm_kernel(pm_ref, gc_ref, gr_ref, brow_ref, bcol_ref, norm_ref, c0_ref, n0_ref, m0_ref,
                  y_ref, c_ref, n_ref, m_ref, c_s, n_s, m_s, *, L):
    t = pl.program_id(1)

    @pl.when(t == 0)
    def _():
        c_s[...] = c0_ref[0]
        n_s[...] = n0_ref[0]
        m_s[...] = m0_ref[0]

    r = lax.broadcasted_iota(jnp.int32, (L, L), 0)
    c = lax.broadcasted_iota(jnp.int32, (L, L), 1)
    causal = r >= c
    tri_l = jnp.where(causal, 1.0, 0.0).astype(BF16)
    tri_u = jnp.where(r <= c, 1.0, 0.0).astype(BF16)

    pre_c = gc_ref[0] + brow_ref[...]
    b_c = _cumsum_rows(tri_l, _log_sigmoid(pre_c))
    pre_r = gr_ref[0] + bcol_ref[...]
    b_r = _cumsum_lanes(_log_sigmoid(pre_r), tri_u)

    scale = HEAD_DIM ** -0.5
    for h in range(H_MLSTM):
        sl = slice(h * HEAD_DIM, (h + 1) * HEAD_DIM)
        q = pm_ref[0, :, sl]
        k = pm_ref[0, :, W_MLSTM + h * HEAD_DIM:W_MLSTM + (h + 1) * HEAD_DIM] * scale
        v = pm_ref[0, :, 2 * W_MLSTM + h * HEAD_DIM:2 * W_MLSTM + (h + 1) * HEAD_DIM]
        og = pm_ref[0, :, 3 * W_MLSTM + h * HEAD_DIM:3 * W_MLSTM + (h + 1) * HEAD_DIM]
        i_col = pre_c[:, h:h + 1]
        b_col = b_c[:, H_MLSTM + h:H_MLSTM + h + 1]
        i_row = pre_r[h:h + 1, :]
        b_row = b_r[H_MLSTM + h:H_MLSTM + h + 1, :]
        m_prev = m_s[:, h:h + 1]
        cmat = c_s[h]
        n_row = n_s[h:h + 1, :]

        log_d = jnp.where(causal, b_col - b_row + i_row, -jnp.inf)
        m_inter = b_col + m_prev
        m_t = jnp.maximum(m_inter, jnp.max(log_d, axis=-1, keepdims=True))
        qb = q.astype(BF16)
        kb = k.astype(BF16)
        vb = v.astype(BF16)
        s = _dot_nt(qb, kb) * jnp.exp(log_d - m_t)
        g = jnp.exp(m_inter - m_t)
        num = g * jnp.dot(qb, cmat.astype(BF16), preferred_element_type=F32) \
            + jnp.dot(s.astype(BF16), vb, preferred_element_type=F32)
        den = g * jnp.sum(q * n_row, axis=-1, keepdims=True) + jnp.sum(s, axis=-1, keepdims=True)
        hh = num / jnp.maximum(jnp.abs(den), jnp.exp(-m_t))

        z = hh * _sigmoid(og)
        y = z * lax.rsqrt(jnp.mean(z * z, axis=-1, keepdims=True) + EPS) * norm_ref[:, sl]
        y_ref[0, :, sl] = y.astype(y_ref.dtype)

        b_last = b_row[:, L - 1:L]
        m_new = jnp.maximum(b_last + m_prev,
                            jnp.max(b_last - b_row + i_row, axis=-1, keepdims=True))
        decay = jnp.exp(b_last + m_prev - m_new)
        kw = k * jnp.exp(b_last - b_col + i_col - m_new)
        c_s[h] = decay * cmat + _dot_tn(kw.astype(BF16), vb)
        n_s[h:h + 1, :] = decay * n_row + jnp.sum(kw, axis=0, keepdims=True)
        m_s[:, h:h + 1] = m_new

    @pl.when(t == pl.num_programs(1) - 1)
    def _():
        c_ref[0] = c_s[...]
        n_ref[0] = n_s[...]
        m_ref[0] = m_s[...]


def _mlstm(pm, gates, gates_t, bias_row, bias_col, norm, c0, n0, m0, L):
    B, T, _ = pm.shape
    tok = lambda b, t: (b, t, 0)
    per_b3 = lambda b, t: (b, 0, 0)
    per_b4 = lambda b, t: (b, 0, 0, 0)
    const = lambda b, t: (0, 0)
    return pl.pallas_call(
        functools.partial(_mlstm_kernel, L=L),
        grid=(B, T // L),
        in_specs=[pl.BlockSpec((1, L, 4 * W_MLSTM), tok),
                  pl.BlockSpec((1, L, GATE_W), tok),
                  pl.BlockSpec((1, 8, L), lambda b, t: (b, 0, t)),
                  pl.BlockSpec((1, GATE_W), const),
                  pl.BlockSpec((8, 1), const),
                  pl.BlockSpec((1, W_MLSTM), const),
                  pl.BlockSpec((1, H_MLSTM, HEAD_DIM, HEAD_DIM), per_b4),
                  pl.BlockSpec((1, H_MLSTM, HEAD_DIM), per_b3),
                  pl.BlockSpec((1, 1, GATE_W), per_b3)],
        out_specs=[pl.BlockSpec((1, L, W_MLSTM), tok),
                   pl.BlockSpec((1, H_MLSTM, HEAD_DIM, HEAD_DIM), per_b4),
                   pl.BlockSpec((1, H_MLSTM, HEAD_DIM), per_b3),
                   pl.BlockSpec((1, 1, GATE_W), per_b3)],
        out_shape=[jax.ShapeDtypeStruct((B, T, W_MLSTM), BF16),
                   jax.ShapeDtypeStruct((B, H_MLSTM, HEAD_DIM, HEAD_DIM), F32),
                   jax.ShapeDtypeStruct((B, H_MLSTM, HEAD_DIM), F32),
                   jax.ShapeDtypeStruct((B, 1, GATE_W), F32)],
        scratch_shapes=[pltpu.VMEM((H_MLSTM, HEAD_DIM, HEAD_DIM), F32),
                        pltpu.VMEM((H_MLSTM, HEAD_DIM), F32),
                        pltpu.VMEM((1, GATE_W), F32)],
        compiler_params=_cparams(("arbitrary", "arbitrary")),
        name="mlstm",
    )(pm, gates, gates_t, bias_row, bias_col, norm, c0, n0, m0)


def _hgrn_levels(L):
    n_lev = int(math.log2(L))
    assert 1 << n_lev == L
    r = np.arange(L)
    sel = np.zeros((n_lev, L, L), np.float32)
    msk = np.zeros((n_lev, L, L), np.float32)
    for lev in range(n_lev):
        w = 1 << lev
        mid = (r // (2 * w)) * (2 * w) + w
        sel[lev, r, mid - 1] = 1.0
        same = (r[:, None] // (2 * w)) == (r[None, :] // (2 * w))
        msk[lev] = same & ((r[:, None] % (2 * w)) >= w) & ((r[None, :] % (2 * w)) < w)
    return (jnp.asarray(sel.reshape(n_lev * L, L), BF16),
            jnp.asarray(np.concatenate([msk, msk], axis=1), F32))


def _hgrn_kernel(ph_ref, llb_ref, l1m_ref, norm_ref, s0_ref, sel_ref, msk_ref, y_ref, s_ref, st_s,
                 *, L, NL):
    t = pl.program_id(1)
    HP = 2 * HEAD_DIM

    @pl.when(t == 0)
    def _():
        st_s[...] = s0_ref[0]

    r = lax.broadcasted_iota(jnp.int32, (L, L), 0)
    c = lax.broadcasted_iota(jnp.int32, (L, L), 1)
    tri_l = jnp.where(r >= c, 1.0, 0.0).astype(BF16)
    head0 = lax.broadcasted_iota(jnp.int32, (L, HP), 1) < HEAD_DIM
    br = lax.broadcasted_iota(jnp.int32, (HP, HP), 0) // HEAD_DIM
    bc = lax.broadcasted_iota(jnp.int32, (HP, HP), 1) // HEAD_DIM
    blk = jnp.where(br == bc, 1.0, 0.0)
    blk_b = blk.astype(BF16)

    def head_sum(x):
        hi = x.astype(BF16)
        lo = (x - hi.astype(F32)).astype(BF16)
        both = jnp.dot(jnp.concatenate([hi, lo], axis=0), blk_b, preferred_element_type=F32)
        return both[:L] + both[L:]

    for p in range(H_HGRN // 2):
        sl = slice(p * HP, (p + 1) * HP)
        q = ph_ref[0, :, sl]
        xf = ph_ref[0, :, W_HGRN + p * HP:W_HGRN + (p + 1) * HP]
        vi = ph_ref[0, :, 2 * W_HGRN + p * HP:2 * W_HGRN + (p + 1) * HP]
        gg = ph_ref[0, :, 3 * W_HGRN + p * HP:3 * W_HGRN + (p + 1) * HP]
        st = st_s[p]

        a = llb_ref[:, sl]
        cc = l1m_ref[:, sl] + _log_sigmoid(xf)
        mx = jnp.maximum(a, cc)
        lf = mx + jnp.log(jnp.exp(a - mx) + jnp.exp(cc - mx))
        kk = jnp.exp(l1m_ref[:, sl]) / (1.0 + jnp.exp(xf))
        b = _cumsum_rows(tri_l, lf)
        b_parts = _split3(b)
        vib = vi.astype(BF16)

        o = _dot_nt((q * jnp.exp(b)).astype(BF16), st.astype(BF16)) + head_sum(q * kk) * vi
        ref_all = sum(jnp.dot(sel_ref[...], part, preferred_element_type=F32) for part in b_parts)
        a01 = jnp.zeros((2 * L, L), F32)
        for lev in range(NL):
            ref = ref_all[lev * L:(lev + 1) * L]
            q_t = q * jnp.exp(jnp.minimum(b - ref, 0.0))
            k_t = (kk * jnp.exp(jnp.minimum(ref - b, 0.0))).astype(BF16)
            q_01 = jnp.concatenate([jnp.where(head0, q_t, 0.0), jnp.where(head0, 0.0, q_t)], axis=0)
            a01 = a01 + msk_ref[lev] * _dot_nt(q_01.astype(BF16), k_t)
        av = jnp.dot(a01.astype(BF16), vib, preferred_element_type=F32)
        o = o + jnp.where(head0, av[:L], av[L:])

        y = o * lax.rsqrt(head_sum(o * o) * (1.0 / HEAD_DIM) + EPS) * norm_ref[:, sl]
        y = y * (gg * _sigmoid(gg))
        y_ref[0, :, sl] = y.astype(y_ref.dtype)

        b_last = b[L - 1:L]
        kd = kk * jnp.exp(b_last - b)
        st_s[p] = st * jnp.exp(b_last) + blk * _dot_tn(vib, kd.astype(BF16))

    @pl.when(t == pl.num_programs(1) - 1)
    def _():
        s_ref[0] = st_s[...]


def _hgrn(ph, log_lb, log1m_lb, norm, s0, L):
    B, T, _ = ph.shape
    HP = 2 * HEAD_DIM
    NP = H_HGRN // 2
    sel, msk = _hgrn_levels(L)
    NL = msk.shape[0]
    s0t = jnp.swapaxes(s0, -1, -2).reshape(B, NP, 2, HEAD_DIM, HEAD_DIM)
    s0bd = jnp.einsum('bpavk,ac->bpavck', s0t, jnp.eye(2, dtype=F32)).reshape(B, NP, HP, HP)
    tok = lambda b, t: (b, t, 0)
    per_b4 = lambda b, t: (b, 0, 0, 0)
    const = lambda b, t: (0, 0)
    const3 = lambda b, t: (0, 0, 0)
    y, st = pl.pallas_call(
        functools.partial(_hgrn_kernel, L=L, NL=NL),
        grid=(B, T // L),
        in_specs=[pl.BlockSpec((1, L, 4 * W_HGRN), tok),
                  pl.BlockSpec((1, W_HGRN), const),
                  pl.BlockSpec((1, W_HGRN), const),
                  pl.BlockSpec((1, W_HGRN), const),
                  pl.BlockSpec((1, NP, HP, HP), per_b4),
                  pl.BlockSpec((NL * L, L), const),
                  pl.BlockSpec((NL, 2 * L, L), const3)],
        out_specs=[pl.BlockSpec((1, L, W_HGRN), tok),
                   pl.BlockSpec((1, NP, HP, HP), per_b4)],
        out_shape=[jax.ShapeDtypeStruct((B, T, W_HGRN), BF16),
                   jax.ShapeDtypeStruct((B, NP, HP, HP), F32)],
        scratch_shapes=[pltpu.VMEM((NP, HP, HP), F32)],
        compiler_params=_cparams(("arbitrary", "arbitrary")),
        name="hgrn2",
    )(ph, log_lb, log1m_lb, norm, s0bd, sel, msk)
    st6 = st.reshape(B, NP, 2, HEAD_DIM, 2, HEAD_DIM)
    st_heads = jnp.stack([st6[:, :, 0, :, 0, :], st6[:, :, 1, :, 1, :]], axis=2)
    return y, jnp.swapaxes(st_heads.reshape(B, H_HGRN, HEAD_DIM, HEAD_DIM), -1, -2)


def _topk_mask(gate, n_valid, lane, n_max):
    cnt = jnp.zeros(gate.shape, F32)
    for n in range(n_max):
        col = gate[:, n:n + 1]
        ahead = jnp.where(lane > n, jnp.where(col >= gate, 1.0, 0.0), jnp.where(col > gate, 1.0, 0.0))
        cnt = cnt + ahead * jnp.where(n < n_valid, 1.0, 0.0)
    return jnp.where(cnt < MOBA_TOPK, jnp.where(lane < n_valid, 1.0, 0.0), 0.0)


def _moba_prompt_kernel(q_ref, k_ref, v_ref, y_ref, km_s, kb_s, vt_s, s_s, *, NB):
    j = pl.program_id(2)
    BS = MOBA_BLOCK

    @pl.when(j == 0)
    def _():
        km_s[...] = jnp.zeros(km_s.shape, F32)
        for n in range(NB):
            kblk = k_ref[0, n * BS:(n + 1) * BS, :]
            km_s[n:n + 1, :] = jnp.mean(kblk, axis=0, keepdims=True)
            kb_s[n] = kblk.astype(BF16)
            vt_s[n] = jnp.transpose(v_ref[0, n * BS:(n + 1) * BS, :]).astype(BF16)

    sub = lax.broadcasted_iota(jnp.int32, (8, BS), 0)
    ki = lax.broadcasted_iota(jnp.int32, (BS, BS), 0)
    qi = lax.broadcasted_iota(jnp.int32, (BS, BS), 1)

    def tile(cap):
        outs = []
        for h in range(2):
            sl = slice(h * HEAD_DIM, (h + 1) * HEAD_DIM)
            q = q_ref[0, :, sl]
            gate = lax.dot_general(km_s[:, sl], q, (((1,), (1,)), ((), ())),
                                   precision=lax.Precision.HIGHEST, preferred_element_type=F32)
            cnt = jnp.zeros((8, BS), F32)
            for n in range(cap):
                row = gate[n:n + 1, :]
                ahead = jnp.where(sub > n, jnp.where(row >= gate, 1.0, 0.0), jnp.where(row > gate, 1.0, 0.0))
                cnt = cnt + ahead * jnp.where(n < j, 1.0, 0.0)
            sel = jnp.where(cnt < MOBA_TOPK, jnp.where(sub < j, 1.0, 0.0), 0.0)
            qb = (q * (HEAD_DIM ** -0.5 * math.log2(math.e))).astype(BF16)

            s = jnp.where(ki <= qi, _dot_nt(kb_s[j, :, sl], qb), NEG_BIG)
            s_s[h, cap] = s
            m = jnp.max(s, axis=0, keepdims=True)
            for n in range(cap):
                s = _dot_nt(kb_s[n, :, sl], qb)
                s_s[h, n] = s
                m = jnp.maximum(m, jnp.where(sel[n:n + 1, :] > 0.0, jnp.max(s, axis=0, keepdims=True), NEG_BIG))
            l = jnp.zeros((1, BS), F32)
            acc = jnp.zeros((HEAD_DIM, BS), F32)
            for n in range(cap + 1):
                off = jnp.where(sel[n:n + 1, :] > 0.0, m, -NEG_BIG) if n < cap else m
                p = jnp.exp2(s_s[h, n] - off)
                l = l + jnp.sum(p, axis=0, keepdims=True)
                vt = vt_s[n, sl, :] if n < cap else vt_s[j, sl, :]
                acc = acc + jnp.dot(vt, p.astype(BF16), preferred_element_type=F32)
            outs.append(acc / l)
        y_ref[0] = jnp.transpose(jnp.concatenate(outs, axis=0)).astype(y_ref.dtype)

    caps = sorted({max(1, NB * i // 4) for i in range(1, 5)})
    lo = 0
    for cap in caps:
        pl.when(jnp.logical_and(j >= lo, j < cap))(functools.partial(tile, cap))
        lo = cap


def _moba_prompt(q, k, v):
    B, T, _ = q.shape
    NB = T // MOBA_BLOCK
    assert NB <= 8
    HP = 2 * HEAD_DIM
    return pl.pallas_call(
        functools.partial(_moba_prompt_kernel, NB=NB),
        grid=(B, W_MOBA // HP, NB),
        in_specs=[pl.BlockSpec((1, MOBA_BLOCK, HP), lambda b, h, j: (b, j, h)),
                  pl.BlockSpec((1, T, HP), lambda b, h, j: (b, 0, h)),
                  pl.BlockSpec((1, T, HP), lambda b, h, j: (b, 0, h))],
        out_specs=pl.BlockSpec((1, MOBA_BLOCK, HP), lambda b, h, j: (b, j, h)),
        out_shape=jax.ShapeDtypeStruct((B, T, W_MOBA), BF16),
        scratch_shapes=[pltpu.VMEM((8, HP), F32),
                        pltpu.VMEM((NB, MOBA_BLOCK, HP), BF16),
                        pltpu.VMEM((NB, HP, MOBA_BLOCK), BF16),
                        pltpu.VMEM((2, NB + 1, MOBA_BLOCK, MOBA_BLOCK), F32)],
        compiler_params=_cparams(("arbitrary", "arbitrary", "arbitrary")),
        name="moba_prompt",
    )(q, k, v)


PAGES_PER_STEP = 8


def _moba_sample_kernel(pt_ref, q_ref, kn_ref, vn_ref, ck_hbm, cv_hbm, y_ref,
                        kbuf, vbuf, sem, gate_s, m_s, l_s, o_s, *, layer, PPS):
    b = pl.program_id(0)
    c = pl.program_id(1)
    n_chunks = pl.num_programs(1)
    step = b * n_chunks + c
    total = pl.num_programs(0) * n_chunks
    slot = step % 2
    nb_per = PPS // 2
    R = q_ref.shape[1]
    KR = MOBA_BLOCK * H_MOBA

    def page_copies(bb, cc, sl):
        out = []
        for p in range(PPS):
            page = pt_ref[bb, cc * PPS + p]
            out.append(pltpu.make_async_copy(ck_hbm.at[page, layer], kbuf.at[sl, p], sem.at[0, sl]))
            out.append(pltpu.make_async_copy(cv_hbm.at[page, layer], vbuf.at[sl, p], sem.at[1, sl]))
        return out

    @pl.when(step == 0)
    def _():
        for cp in page_copies(b, c, slot):
            cp.start()

    @pl.when(step + 1 < total)
    def _():
        nxt = step + 1
        for cp in page_copies(nxt // n_chunks, nxt % n_chunks, 1 - slot):
            cp.start()

    for cp in page_copies(b, c, slot):
        cp.wait()

    @pl.when(c == 0)
    def _():
        gate_s[...] = jnp.zeros(gate_s.shape, F32)
        m_s[...] = jnp.zeros(m_s.shape, F32)
        l_s[...] = jnp.zeros(l_s.shape, F32)

    q = q_ref[0]
    q_s = q * (HEAD_DIM ** -0.5)
    q_b = q_s.astype(BF16)
    own_head = (lax.broadcasted_iota(jnp.int32, (R, KR), 0) % H_MOBA
                == lax.broadcasted_iota(jnp.int32, (R, KR), 1) % H_MOBA)
    lane = lax.broadcasted_iota(jnp.int32, (R, 128), 1)

    for i in range(nb_per):
        kblk = kbuf[slot, 2 * i:2 * i + 2].reshape(KR, HEAD_DIM)
        vblk = vbuf[slot, 2 * i:2 * i + 2].reshape(KR, HEAD_DIM)
        n = c * nb_per + i
        kmean = jnp.sum(kblk.reshape(MOBA_BLOCK, H_MOBA, HEAD_DIM), axis=0) * (1.0 / MOBA_BLOCK)
        gcol = jnp.sum(q.reshape(R // H_MOBA, H_MOBA, HEAD_DIM) * kmean[None], axis=-1,
                       keepdims=True).reshape(R, 1)
        s = jnp.where(own_head, _dot_nt(q_b, kblk.astype(BF16)), NEG_BIG)
        mcol = jnp.max(s, axis=-1, keepdims=True)
        p = jnp.exp(s - mcol)
        lcol = jnp.sum(p, axis=-1, keepdims=True)
        o_s[n] = jnp.dot(p.astype(BF16), vblk.astype(BF16), preferred_element_type=F32)
        hit = lane == n
        gate_s[...] = jnp.where(hit, gcol, gate_s[...])
        m_s[...] = jnp.where(hit, mcol, m_s[...])
        l_s[...] = jnp.where(hit, lcol, l_s[...])

    @pl.when(c == n_chunks - 1)
    def _():
        n_blk = o_s.shape[0]
        sel = _topk_mask(gate_s[...], n_blk, lane, n_blk) > 0.0
        ri = lax.broadcasted_iota(jnp.int32, (R, R), 0)
        ci = lax.broadcasted_iota(jnp.int32, (R, R), 1)
        own_ok = jnp.logical_and(ri % H_MOBA == ci % H_MOBA, ci // H_MOBA <= ri // H_MOBA)
        s_own = lax.dot_general(q_s, kn_ref[0], (((1,), (1,)), ((), ())),
                                precision=lax.Precision.HIGHEST, preferred_element_type=F32)
        s_own = jnp.where(own_ok, s_own, NEG_BIG)
        m_tot = jnp.maximum(jnp.max(jnp.where(sel, m_s[...], NEG_BIG), axis=-1, keepdims=True),
                            jnp.max(s_own, axis=-1, keepdims=True))
        w = jnp.where(sel, jnp.exp(m_s[...] - m_tot), 0.0)
        p_own = jnp.exp(s_own - m_tot)
        den = jnp.sum(w * l_s[...], axis=-1, keepdims=True) + jnp.sum(p_own, axis=-1, keepdims=True)
        acc = lax.dot_general(p_own, vn_ref[0], (((1,), (0,)), ((), ())),
                              precision=lax.Precision.HIGHEST, preferred_element_type=F32)
        for n in range(n_blk):
            acc = acc + w[:, n:n + 1] * o_s[n]
        y_ref[0] = (acc / den).astype(y_ref.dtype)


def _moba_sample(q, kn, vn, cache_k, cache_v, page_table, layer):
    B, R, _ = q.shape
    n_pages = page_table.shape[1]
    PPS = min(PAGES_PER_STEP, n_pages)
    n_chunks = n_pages // PPS
    n_blk = n_pages * PAGE_SIZE // MOBA_BLOCK
    assert n_blk <= 128
    per_b = lambda b, c, pt: (b, 0, 0)
    grid_spec = pltpu.PrefetchScalarGridSpec(
        num_scalar_prefetch=1,
        grid=(B, n_chunks),
        in_specs=[pl.BlockSpec((1, R, HEAD_DIM), per_b),
                  pl.BlockSpec((1, R, HEAD_DIM), per_b),
                  pl.BlockSpec((1, R, HEAD_DIM), per_b),
                  pl.BlockSpec(memory_space=pl.ANY),
                  pl.BlockSpec(memory_space=pl.ANY)],
        out_specs=pl.BlockSpec((1, R, HEAD_DIM), per_b),
        scratch_shapes=[pltpu.VMEM((2, PPS, PAGE_SIZE, H_MOBA, HEAD_DIM), F32),
                        pltpu.VMEM((2, PPS, PAGE_SIZE, H_MOBA, HEAD_DIM), F32),
                        pltpu.SemaphoreType.DMA((2, 2)),
                        pltpu.VMEM((R, 128), F32),
                        pltpu.VMEM((R, 128), F32),
                        pltpu.VMEM((R, 128), F32),
                        pltpu.VMEM((n_blk, R, HEAD_DIM), F32)])
    return pl.pallas_call(
        functools.partial(_moba_sample_kernel, layer=layer, PPS=PPS),
        grid_spec=grid_spec,
        out_shape=jax.ShapeDtypeStruct((B, R, HEAD_DIM), BF16),
        compiler_params=_cparams(("arbitrary", "arbitrary")),
        name="moba_sample",
    )(page_table, q, kn, vn, cache_k, cache_v)


def _outproj_kernel(x_ref, ya_ref, yb_ref, yc_ref, w_ref, g_ref, o_ref):
    a0, a1 = W_MLSTM, W_MLSTM + W_MOBA
    mix = (jnp.dot(ya_ref[...], w_ref[0:a0, :], preferred_element_type=F32)
           + jnp.dot(yb_ref[...], w_ref[a0:a1, :], preferred_element_type=F32)
           + jnp.dot(yc_ref[...], w_ref[a1:, :], preferred_element_type=F32))
    o_ref[...] = x_ref[...] + mix * lax.rsqrt(jnp.mean(mix * mix, axis=-1, keepdims=True) + EPS) * g_ref[...]


def _outproj(x2d, ya, yb, yc, w, g, tm):
    n = x2d.shape[0]
    row = lambda i: (i, 0)
    const = lambda i: (0, 0)
    return pl.pallas_call(
        _outproj_kernel,
        grid=(n // tm,),
        in_specs=[pl.BlockSpec((tm, D_MODEL), row),
                  pl.BlockSpec((tm, W_MLSTM), row),
                  pl.BlockSpec((tm, W_MOBA), row),
                  pl.BlockSpec((tm, W_HGRN), row),
                  pl.BlockSpec((D_MODEL, D_MODEL), const),
                  pl.BlockSpec((1, D_MODEL), const)],
        out_specs=pl.BlockSpec((tm, D_MODEL), row),
        out_shape=jax.ShapeDtypeStruct((n, D_MODEL), F32),
        compiler_params=_cparams(("arbitrary",)),
        name="outproj",
    )(x2d, ya, yb, yc, w, g)


FF_CHUNKS = 2
FF_CHUNK = D_FF // FF_CHUNKS


def _gelu_tanh(x):
    return 0.5 * x * (1.0 + jnp.tanh(math.sqrt(2.0 / math.pi) * (x + 0.044715 * (x * x * x))))


def _ffn_kernel(x_ref, g2_ref, wu_ref, wg_ref, wd_ref, cw_ref, cb_ref, c0_ref, g3_ref,
                o_ref, tail_ref, hb_s, acc_s, ext_s, *, tm, R, S):
    tt = pl.program_id(1)
    c = pl.program_id(2)

    @pl.when(c == 0)
    def _():
        x = x_ref[...]
        h = x * lax.rsqrt(jnp.mean(x * x, axis=-1, keepdims=True) + EPS) * g2_ref[...]
        hb_s[...] = h.astype(BF16)

    @pl.when(tt == 0)
    def _():
        ext_s[c, 0:R, :] = c0_ref[0]

    hb = hb_s[...]
    u = jnp.dot(hb, wu_ref[...], preferred_element_type=F32)
    g = jnp.dot(hb, wg_ref[...], preferred_element_type=F32)
    ext_s[c, R:R + tm, :] = u
    uc = (cb_ref[...]
          + cw_ref[0:1, :] * ext_s[c, R - 2 * S:R - 2 * S + tm, :]
          + cw_ref[1:2, :] * ext_s[c, R - S:R - S + tm, :]
          + cw_ref[2:3, :] * u)
    act = (_gelu_tanh(uc) * g).astype(BF16)
    d = jnp.dot(act, wd_ref[...], preferred_element_type=F32)

    @pl.when(c == 0)
    def _():
        acc_s[...] = d

    @pl.when(c > 0)
    def _():
        acc_s[...] += d

    tail = ext_s[c, tm:tm + R, :]
    tail_ref[0, c] = tail
    ext_s[c, 0:R, :] = tail

    @pl.when(c == pl.num_programs(2) - 1)
    def _():
        a = acc_s[...]
        o_ref[...] = x_ref[...] + a * lax.rsqrt(jnp.mean(a * a, axis=-1, keepdims=True) + EPS) * g3_ref[...]


def _ffn(x2d, g2, w_up, w_down, conv_w8, conv_b, carry0, g3, *, nseq, tm, R, S):
    n = x2d.shape[0]
    ntt = n // (nseq * tm)
    row = lambda s, t, c: (s * ntt + t, 0)
    const = lambda s, t, c: (0, 0)
    x_new, tail = pl.pallas_call(
        functools.partial(_ffn_kernel, tm=tm, R=R, S=S),
        grid=(nseq, ntt, FF_CHUNKS),
        in_specs=[pl.BlockSpec((tm, D_MODEL), row),
                  pl.BlockSpec((1, D_MODEL), const),
                  pl.BlockSpec((D_MODEL, FF_CHUNK), lambda s, t, c: (0, c)),
                  pl.BlockSpec((D_MODEL, FF_CHUNK), lambda s, t, c: (0, FF_CHUNKS + c)),
                  pl.BlockSpec((FF_CHUNK, D_MODEL), lambda s, t, c: (c, 0)),
                  pl.BlockSpec((8, FF_CHUNK), lambda s, t, c: (0, c)),
                  pl.BlockSpec((1, FF_CHUNK), lambda s, t, c: (0, c)),
                  pl.BlockSpec((1, R, FF_CHUNK), lambda s, t, c: (s, 0, c)),
                  pl.BlockSpec((1, D_MODEL), const)],
        out_specs=[pl.BlockSpec((tm, D_MODEL), row),
                   pl.BlockSpec((1, FF_CHUNKS, R, FF_CHUNK), lambda s, t, c: (s, 0, 0, 0))],
        out_shape=[jax.ShapeDtypeStruct((n, D_MODEL), F32),
                   jax.ShapeDtypeStruct((nseq, FF_CHUNKS, R, FF_CHUNK), F32)],
        scratch_shapes=[pltpu.VMEM((tm, D_MODEL), BF16),
                        pltpu.VMEM((tm, D_MODEL), F32),
                        pltpu.VMEM((FF_CHUNKS, R + tm, FF_CHUNK), F32)],
        compiler_params=_cparams(("arbitrary", "arbitrary", "arbitrary")),
        name="convffn",
    )(x2d, g2, w_up, w_up, w_down, conv_w8, conv_b, carry0, g3)
    return x_new, jnp.transpose(tail, (0, 2, 1, 3)).reshape(nseq, R, D_FF)


def _rope_tables(pos):
    half = HEAD_DIM // 2
    inv = ROPE_THETA ** (-jnp.arange(half, dtype=F32) / half)
    ang = pos.astype(F32)[:, None] * inv[None, :]
    cos = jnp.cos(ang)
    sin = jnp.sin(ang)
    cos_t = jnp.tile(jnp.concatenate([cos, cos], axis=-1), (1, H_MOBA))
    sin_t = jnp.tile(jnp.concatenate([-sin, sin], axis=-1), (1, H_MOBA))
    return cos_t, sin_t


def _layer_weights(l, w_in_r, gate_bias, mlstm_norm, lower_bounds, hgrn_norm, w_out_b, norms,
                   w_up_b, conv_w, conv_b, w_down_b):
    bias = gate_bias[l].reshape(2 * H_MLSTM)
    lb = lower_bounds[l]
    return dict(
        w_in=w_in_r[l], w_out=w_out_b[l], w_up=w_up_b[l], w_down=w_down_b[l],
        g0=norms[l, 0][None], g1=norms[l, 1][None], g2=norms[l, 2][None], g3=norms[l, 3][None],
        bias_row=jnp.pad(bias, (0, GATE_W - 2 * H_MLSTM))[None],
        bias_col=bias[:, None],
        m_norm=mlstm_norm[l][None], h_norm=hgrn_norm[l][None],
        log_lb=jnp.log(lb)[None], log1m_lb=jnp.log1p(-lb)[None],
        conv_w8=jnp.pad(conv_w[l], ((0, 8 - CONV_W), (0, 0))), conv_b=conv_b[l][None])


def _layer_prompt(x2d, B, T, lw, cos, sin):
    pm, q, k, v, ph, gt = _inproj(x2d, lw["g0"], lw["w_in"], cos, sin, 256)
    r3 = lambda a: a.reshape(B, T, a.shape[-1])
    gates = r3(gt)
    gates_t = jnp.transpose(gates[:, :, :2 * H_MLSTM], (0, 2, 1))
    c0 = jnp.zeros((B, H_MLSTM, HEAD_DIM, HEAD_DIM), F32)
    n0 = jnp.zeros((B, H_MLSTM, HEAD_DIM), F32)
    m0 = jnp.zeros((B, 1, GATE_W), F32)
    ya, c_new, n_new, m_new = _mlstm(r3(pm), gates, gates_t, lw["bias_row"], lw["bias_col"],
                                     lw["m_norm"], c0, n0, m0, min(128, T))
    yb = _moba_prompt(r3(q), r3(k), r3(v))
    s0 = jnp.zeros((B, H_HGRN, HEAD_DIM, HEAD_DIM), F32)
    yc, s_new = _hgrn(r3(ph), lw["log_lb"], lw["log1m_lb"], lw["h_norm"], s0, min(128, T))
    r2 = lambda a: a.reshape(B * T, a.shape[-1])
    x1 = _outproj(x2d, r2(ya), r2(yb), r2(yc), lw["w_out"], lw["g1"], 512)
    carry0 = jnp.zeros((B, 8, D_FF), F32)
    x2, tail = _ffn(x1, lw["g2"], lw["w_up"], lw["w_down"], lw["conv_w8"], lw["conv_b"], carry0,
                    lw["g3"], nseq=B, tm=min(512, T), R=8, S=1)
    return (x2, k, v, c_new, n_new, m_new[:, 0, :H_MLSTM], s_new, tail[:, 8 - (CONV_W - 1):])


SAMPLE_CHUNK = 8


def _layer_sample(x2d, B, T, lw, cos, sin, cache_k, cache_v, page_table, layer, c0, n0, m0, s0, conv0):
    pm, q, k, v, ph, gt = _inproj(x2d, lw["g0"], lw["w_in"], cos, sin, B * T)
    L = SAMPLE_CHUNK

    def seq_major(a, pad_row=None):
        a = jnp.transpose(a.reshape(T, B, a.shape[-1]), (1, 0, 2))
        if pad_row is None:
            return a
        fill = jnp.broadcast_to(pad_row[None, None, :], (B, L - T, a.shape[-1]))
        return jnp.concatenate([a, fill], axis=1)

    def time_major(a):
        return jnp.transpose(a[:, :T], (1, 0, 2)).reshape(T * B, a.shape[-1])

    gate_pad = jnp.where(jnp.arange(GATE_W) < H_MLSTM, NEG_BIG, -NEG_BIG).astype(F32)
    gates = seq_major(gt, gate_pad)
    gates_t = jnp.transpose(gates[:, :, :2 * H_MLSTM], (0, 2, 1))
    m0p = jnp.pad(m0, ((0, 0), (0, GATE_W - H_MLSTM)))[:, None, :]
    ya, c_new, n_new, m_new = _mlstm(seq_major(pm, jnp.zeros((4 * W_MLSTM,), F32)), gates, gates_t,
                                     lw["bias_row"], lw["bias_col"], lw["m_norm"], c0, n0, m0p, L)
    col = jnp.arange(4 * W_HGRN)
    h_pad = jnp.where((col >= W_HGRN) & (col < 2 * W_HGRN), -NEG_BIG, 0.0).astype(F32)
    yc, s_new = _hgrn(seq_major(ph, h_pad), lw["log_lb"], lw["log1m_lb"], lw["h_norm"], s0, L)
    heads = lambda a: seq_major(a).reshape(B, T * H_MOBA, HEAD_DIM)
    yb = _moba_sample(heads(q), heads(k), heads(v), cache_k, cache_v, page_table, layer)
    yb = yb.reshape(B, T, W_MOBA)
    x1 = _outproj(x2d, time_major(ya), time_major(yb), time_major(yc), lw["w_out"], lw["g1"], B * T)
    carry0 = jnp.transpose(conv0, (1, 0, 2)).reshape(1, (CONV_W - 1) * B, D_FF)
    x2, tail = _ffn(x1, lw["g2"], lw["w_up"], lw["w_down"], lw["conv_w8"], lw["conv_b"], carry0,
                    lw["g3"], nseq=1, tm=B * T, R=(CONV_W - 1) * B, S=B)
    conv_new = jnp.transpose(tail.reshape(CONV_W - 1, B, D_FF), (1, 0, 2))
    tb = lambda a: jnp.transpose(a.reshape(T, B, H_MOBA, HEAD_DIM), (1, 0, 2, 3))
    return (x2, tb(k), tb(v), c_new, n_new, m_new[:, 0, :H_MLSTM], s_new, conv_new)


def kernel(x_prompt, x_sample, cache_k, cache_v, page_table, state_mlstm_c, state_mlstm_n,
           state_mlstm_m, state_hgrn, state_conv, w_in, mlstm_gate_bias, mlstm_norm,
           hgrn_lb_logits, hgrn_norm, w_out, norms, ffn_w_up, ffn_conv_w, ffn_conv_b, ffn_w_down):
    Bp, Tp, _ = x_prompt.shape
    Bs, Ts, _ = x_sample.shape
    depth = w_in.shape[0]
    past_len = page_table.shape[1] * PAGE_SIZE

    lb_cs = jnp.cumsum(jax.nn.softmax(hgrn_lb_logits.astype(F32), axis=0), axis=0)
    lower_bounds = lb_cs - lb_cs[0:1]
    g0 = 4 * W_MLSTM
    g1 = g0 + 2 * H_MLSTM
    w_in_r = jnp.concatenate(
        [w_in[:, :, :g0], w_in[:, :, g1:], w_in[:, :, g0:g1],
         jnp.zeros((depth, D_MODEL, GATE_W - 2 * H_MLSTM), w_in.dtype)], axis=-1).astype(BF16)
    w_out_b = w_out.astype(BF16)
    w_up_b = ffn_w_up.astype(BF16)
    w_down_b = ffn_w_down.astype(BF16)
    cos_p, sin_p = _rope_tables(jnp.arange(Tp))
    cos_s, sin_s = _rope_tables(past_len + jnp.arange(Ts * Bs) // Bs)

    xp = x_prompt.reshape(Bp * Tp, D_MODEL)
    xs = jnp.transpose(x_sample, (1, 0, 2)).reshape(Ts * Bs, D_MODEL)
    rec_p, rec_s = [], []
    for l in range(depth):
        lw = _layer_weights(l, w_in_r, mlstm_gate_bias, mlstm_norm, lower_bounds, hgrn_norm,
                            w_out_b, norms, w_up_b, ffn_conv_w, ffn_conv_b, w_down_b)
        xp, *new_p = _layer_prompt(xp, Bp, Tp, lw, cos_p, sin_p)
        xs, *new_s = _layer_sample(xs, Bs, Ts, lw, cos_s, sin_s, cache_k, cache_v, page_table, l,
                                   state_mlstm_c[l], state_mlstm_n[l], state_mlstm_m[l],
                                   state_hgrn[l], state_conv[l])
        rec_p.append(new_p)
        rec_s.append(new_s)

    def stack(rec, i, axis):
        return jnp.stack([r[i] for r in rec], axis=axis)

    n_pg = Tp // PAGE_SIZE
    page = lambda a: a.reshape(Bp, n_pg, PAGE_SIZE, H_MOBA, HEAD_DIM)
    k_prompt = jnp.stack([page(r[0]) for r in rec_p], axis=2)
    v_prompt = jnp.stack([page(r[1]) for r in rec_p], axis=2)
    y_prompt = xp.reshape(Bp, Tp, D_MODEL)
    y_sample = jnp.transpose(xs.reshape(Ts, Bs, D_MODEL), (1, 0, 2))
    return (y_prompt, y_sample, k_prompt, v_prompt,
            stack(rec_p, 2, 0), stack(rec_p, 3, 0), stack(rec_p, 4, 0), stack(rec_p, 5, 0),
            stack(rec_p, 6, 0),
            stack(rec_s, 0, 1), stack(rec_s, 1, 1),
            stack(rec_s, 2, 0), stack(rec_s, 3, 0), stack(rec_s, 4, 0), stack(rec_s, 5, 0),
            stack(rec_s, 6, 0))
```

```python
import functools
import math

import jax
import numpy as np
import jax.numpy as jnp
from jax import lax
from jax.experimental import pallas as pl
from jax.experimental.pallas import tpu as pltpu

F32 = jnp.float32
BF16 = jnp.bfloat16

D_MODEL = 1024
HEAD_DIM = 64
H_MLSTM = 4
H_MOBA = 8
H_HGRN = 4
W_MLSTM = H_MLSTM * HEAD_DIM
W_MOBA = H_MOBA * HEAD_DIM
W_HGRN = H_HGRN * HEAD_DIM
MOBA_BLOCK = 256
MOBA_TOPK = 3
ROPE_THETA = 10000.0
D_FF = 2816
CONV_W = 3
EPS = 1e-6
PAGE_SIZE = 128
GATE_W = 128
PROJ_W = 4 * W_MLSTM + 3 * W_MOBA + 4 * W_HGRN + GATE_W
NEG_BIG = -1e30
VMEM_LIMIT = 56 * 1024 * 1024


def _cparams(sem):
    return pltpu.CompilerParams(dimension_semantics=sem, vmem_limit_bytes=VMEM_LIMIT)


def _log_sigmoid(x):
    return jnp.minimum(x, 0.0) - jnp.log1p(jnp.exp(-jnp.abs(x)))


def _sigmoid(x):
    return 1.0 / (1.0 + jnp.exp(-x))


def _split3(x):
    h1 = x.astype(BF16)
    r1 = x - h1.astype(F32)
    h2 = r1.astype(BF16)
    h3 = (r1 - h2.astype(F32)).astype(BF16)
    return h1, h2, h3


def _cumsum_rows(tri, x):
    return sum(jnp.dot(tri, p, preferred_element_type=F32) for p in _split3(x))


def _cumsum_lanes(x, tri_u):
    return sum(jnp.dot(p, tri_u, preferred_element_type=F32) for p in _split3(x))


def _dot_nt(a, b):
    return lax.dot_general(a, b, (((1,), (1,)), ((), ())), preferred_element_type=F32)


def _dot_tn(a, b):
    return lax.dot_general(a, b, (((0,), (0,)), ((), ())), preferred_element_type=F32)


def _inproj_kernel(x_ref, g_ref, w_ref, cos_ref, sin_ref,
                   pm_ref, q_ref, k_ref, v_ref, ph_ref, gt_ref):
    x = x_ref[...]
    h = x * lax.rsqrt(jnp.mean(x * x, axis=-1, keepdims=True) + EPS) * g_ref[...]
    p = jnp.dot(h.astype(BF16), w_ref[...], preferred_element_type=F32)
    o = 4 * W_MLSTM
    pm_ref[...] = p[:, 0:o]
    cos = cos_ref[...]
    sin = sin_ref[...]
    lane = lax.broadcasted_iota(jnp.int32, cos.shape, 1)
    first_half = (lane % HEAD_DIM) < (HEAD_DIM // 2)

    def rope(a):
        swapped = jnp.where(first_half,
                            pltpu.roll(a, W_MOBA - HEAD_DIM // 2, axis=1),
                            pltpu.roll(a, HEAD_DIM // 2, axis=1))
        return a * cos + swapped * sin

    q_ref[...] = rope(p[:, o:o + W_MOBA])
    k_ref[...] = rope(p[:, o + W_MOBA:o + 2 * W_MOBA])
    v_ref[...] = p[:, o + 2 * W_MOBA:o + 3 * W_MOBA]
    o2 = o + 3 * W_MOBA
    ph_ref[...] = p[:, o2:o2 + 4 * W_HGRN]
    gt_ref[...] = p[:, o2 + 4 * W_HGRN:]


def _inproj(x2d, g, w, cos, sin, tm):
    n = x2d.shape[0]
    tab_blocks = cos.shape[0] // tm
    row = lambda i: (i, 0)
    const = lambda i: (0, 0)
    tab = lambda i: (i % tab_blocks, 0)
    widths = (4 * W_MLSTM, W_MOBA, W_MOBA, W_MOBA, 4 * W_HGRN, GATE_W)
    return pl.pallas_call(
        _inproj_kernel,
        grid=(n // tm,),
        in_specs=[pl.BlockSpec((tm, D_MODEL), row),
                  pl.BlockSpec((1, D_MODEL), const),
                  pl.BlockSpec((D_MODEL, PROJ_W), const),
                  pl.BlockSpec((tm, W_MOBA), tab),
                  pl.BlockSpec((tm, W_MOBA), tab)],
        out_specs=[pl.BlockSpec((tm, wd), row) for wd in widths],
        out_shape=[jax.ShapeDtypeStruct((n, wd), F32) for wd in widths],
        compiler_params=_cparams(("arbitrary",)),
        name="inproj",
    )(x2d, g, w, cos, sin)


def _mlstm_kernel(pm_ref, gc_ref, gr_ref, brow_ref, bcol_ref, norm_ref, c0_ref, n0_ref, m0_ref,
                  y_ref, c_ref, n_ref, m_ref, c_s, n_s, m_s, *, L):
    t = pl.program_id(1)

    @pl.when(t == 0)
    def _():
        c_s[...] = c0_ref[0]
        n_s[...] = n0_ref[0]
        m_s[...] = m0_ref[0]

    r = lax.broadcasted_iota(jnp.int32, (L, L), 0)
    c = lax.broadcasted_iota(jnp.int32, (L, L), 1)
    causal = r >= c
    tri_l = jnp.where(causal, 1.0, 0.0).astype(BF16)
    tri_u = jnp.where(r <= c, 1.0, 0.0).astype(BF16)

    pre_c = gc_ref[0] + brow_ref[...]
    b_c = _cumsum_rows(tri_l, _log_sigmoid(pre_c))
    pre_r = gr_ref[0] + bcol_ref[...]
    b_r = _cumsum_lanes(_log_sigmoid(pre_r), tri_u)

    scale = HEAD_DIM ** -0.5
    for h in range(H_MLSTM):
        sl = slice(h * HEAD_DIM, (h + 1) * HEAD_DIM)
        q = pm_ref[0, :, sl]
        k = pm_ref[0, :, W_MLSTM + h * HEAD_DIM:W_MLSTM + (h + 1) * HEAD_DIM] * scale
        v = pm_ref[0, :, 2 * W_MLSTM + h * HEAD_DIM:2 * W_MLSTM + (h + 1) * HEAD_DIM]
        og = pm_ref[0, :, 3 * W_MLSTM + h * HEAD_DIM:3 * W_MLSTM + (h + 1) * HEAD_DIM]
        i_col = pre_c[:, h:h + 1]
        b_col = b_c[:, H_MLSTM + h:H_MLSTM + h + 1]
        i_row = pre_r[h:h + 1, :]
        b_row = b_r[H_MLSTM + h:H_MLSTM + h + 1, :]
        m_prev = m_s[:, h:h + 1]
        cmat = c_s[h]
        n_row = n_s[h:h + 1, :]

        log_d = jnp.where(causal, b_col - b_row + i_row, -jnp.inf)
        m_inter = b_col + m_prev
        m_t = jnp.maximum(m_inter, jnp.max(log_d, axis=-1, keepdims=True))
        qb = q.astype(BF16)
        kb = k.astype(BF16)
        vb = v.astype(BF16)
        s = _dot_nt(qb, kb) * jnp.exp(log_d - m_t)
        g = jnp.exp(m_inter - m_t)
        num = g * jnp.dot(qb, cmat.astype(BF16), preferred_element_type=F32) \
            + jnp.dot(s.astype(BF16), vb, preferred_element_type=F32)
        den = g * jnp.sum(q * n_row, axis=-1, keepdims=True) + jnp.sum(s, axis=-1, keepdims=True)
        hh = num / jnp.maximum(jnp.abs(den), jnp.exp(-m_t))

        z = hh * _sigmoid(og)
        y = z * lax.rsqrt(jnp.mean(z * z, axis=-1, keepdims=True) + EPS) * norm_ref[:, sl]
        y_ref[0, :, sl] = y.astype(y_ref.dtype)

        b_last = b_row[:, L - 1:L]
        m_new = jnp.maximum(b_last + m_prev,
                            jnp.max(b_last - b_row + i_row, axis=-1, keepdims=True))
        decay = jnp.exp(b_last + m_prev - m_new)
        kw = k * jnp.exp(b_last - b_col + i_col - m_new)
        c_s[h] = decay * cmat + _dot_tn(kw.astype(BF16), vb)
        n_s[h:h + 1, :] = decay * n_row + jnp.sum(kw, axis=0, keepdims=True)
        m_s[:, h:h + 1] = m_new

    @pl.when(t == pl.num_programs(1) - 1)
    def _():
        c_ref[0] = c_s[...]
        n_ref[0] = n_s[...]
        m_ref[0] = m_s[...]


def _mlstm(pm, gates, gates_t, bias_row, bias_col, norm, c0, n0, m0, L):
    B, T, _ = pm.shape
    tok = lambda b, t: (b, t, 0)
    per_b3 = lambda b, t: (b, 0, 0)
    per_b4 = lambda b, t: (b, 0, 0, 0)
    const = lambda b, t: (0, 0)
    return pl.pallas_call(
        functools.partial(_mlstm_kernel, L=L),
        grid=(B, T // L),
        in_specs=[pl.BlockSpec((1, L, 4 * W_MLSTM), tok),
                  pl.BlockSpec((1, L, GATE_W), tok),
                  pl.BlockSpec((1, 8, L), lambda b, t: (b, 0, t)),
                  pl.BlockSpec((1, GATE_W), const),
                  pl.BlockSpec((8, 1), const),
                  pl.BlockSpec((1, W_MLSTM), const),
                  pl.BlockSpec((1, H_MLSTM, HEAD_DIM, HEAD_DIM), per_b4),
                  pl.BlockSpec((1, H_MLSTM, HEAD_DIM), per_b3),
                  pl.BlockSpec((1, 1, GATE_W), per_b3)],
        out_specs=[pl.BlockSpec((1, L, W_MLSTM), tok),
                   pl.BlockSpec((1, H_MLSTM, HEAD_DIM, HEAD_DIM), per_b4),
                   pl.BlockSpec((1, H_MLSTM, HEAD_DIM), per_b3),
                   pl.BlockSpec((1, 1, GATE_W), per_b3)],
        out_shape=[jax.ShapeDtypeStruct((B, T, W_MLSTM), BF16),
                   jax.ShapeDtypeStruct((B, H_MLSTM, HEAD_DIM, HEAD_DIM), F32),
                   jax.ShapeDtypeStruct((B, H_MLSTM, HEAD_DIM), F32),
                   jax.ShapeDtypeStruct((B, 1, GATE_W), F32)],
        scratch_shapes=[pltpu.VMEM((H_MLSTM, HEAD_DIM, HEAD_DIM), F32),
                        pltpu.VMEM((H_MLSTM, HEAD_DIM), F32),
                        pltpu.VMEM((1, GATE_W), F32)],
        compiler_params=_cparams(("arbitrary", "arbitrary")),
        name="mlstm",
    )(pm, gates, gates_t, bias_row, bias_col, norm, c0, n0, m0)


def _hgrn_levels(L):
    n_lev = int(math.log2(L))
    assert 1 << n_lev == L
    r = np.arange(L)
    sel = np.zeros((n_lev, L, L), np.float32)
    msk = np.zeros((n_lev, L, L), np.float32)
    for lev in range(n_lev):
        w = 1 << lev
        mid = (r // (2 * w)) * (2 * w) + w
        sel[lev, r, mid - 1] = 1.0
        same = (r[:, None] // (2 * w)) == (r[None, :] // (2 * w))
        msk[lev] = same & ((r[:, None] % (2 * w)) >= w) & ((r[None, :] % (2 * w)) < w)
    return (jnp.asarray(sel.reshape(n_lev * L, L), BF16),
            jnp.asarray(np.concatenate([msk, msk], axis=1), F32))


def _hgrn_kernel(ph_ref, llb_ref, l1m_ref, norm_ref, s0_ref, sel_ref, msk_ref, y_ref, s_ref, st_s,
                 *, L, NL):
    t = pl.program_id(1)
    HP = 2 * HEAD_DIM

    @pl.when(t == 0)
    def _():
        st_s[...] = s0_ref[0]

    r = lax.broadcasted_iota(jnp.int32, (L, L), 0)
    c = lax.broadcasted_iota(jnp.int32, (L, L), 1)
    tri_l = jnp.where(r >= c, 1.0, 0.0).astype(BF16)
    head0 = lax.broadcasted_iota(jnp.int32, (L, HP), 1) < HEAD_DIM
    br = lax.broadcasted_iota(jnp.int32, (HP, HP), 0) // HEAD_DIM
    bc = lax.broadcasted_iota(jnp.int32, (HP, HP), 1) // HEAD_DIM
    blk = jnp.where(br == bc, 1.0, 0.0)
    blk_b = blk.astype(BF16)

    def head_sum(x):
        hi = x.astype(BF16)
        lo = (x - hi.astype(F32)).astype(BF16)
        both = jnp.dot(jnp.concatenate([hi, lo], axis=0), blk_b, preferred_element_type=F32)
        return both[:L] + both[L:]

    pairs = range(H_HGRN // 2)
    sls = [slice(p * HP, (p + 1) * HP) for p in pairs]
    qs = [ph_ref[0, :, sls[p]] for p in pairs]
    vis = [ph_ref[0, :, 2 * W_HGRN + p * HP:2 * W_HGRN + (p + 1) * HP] for p in pairs]
    vibs = [vi.astype(BF16) for vi in vis]
    sts = [st_s[p] for p in pairs]

    lfs, kks, bs = [], [], []
    for p in pairs:
        xf = ph_ref[0, :, W_HGRN + p * HP:W_HGRN + (p + 1) * HP]
        a = llb_ref[:, sls[p]]
        cc = l1m_ref[:, sls[p]] + _log_sigmoid(xf)
        mx = jnp.maximum(a, cc)
        lfs.append(mx + jnp.log(jnp.exp(a - mx) + jnp.exp(cc - mx)))
        kks.append(jnp.exp(l1m_ref[:, sls[p]]) / (1.0 + jnp.exp(xf)))
    for p in pairs:
        bs.append(_cumsum_rows(tri_l, lfs[p]))

    refs, outs = [], []
    for p in pairs:
        q, kk, b = qs[p], kks[p], bs[p]
        refs.append(sum(jnp.dot(sel_ref[...], part, preferred_element_type=F32) for part in _split3(b)))
        outs.append(_dot_nt((q * jnp.exp(b)).astype(BF16), sts[p].astype(BF16))
                    + head_sum(q * kk) * vis[p])
        b_last = b[L - 1:L]
        kd = kk * jnp.exp(b_last - b)
        st_s[p] = sts[p] * jnp.exp(b_last) + blk * _dot_tn(vibs[p], kd.astype(BF16))

    a01s = [jnp.zeros((2 * L, L), F32) for _ in pairs]
    for lev in range(NL):
        for p in pairs:
            q, kk, b = qs[p], kks[p], bs[p]
            ref = refs[p][lev * L:(lev + 1) * L]
            q_t = q * jnp.exp(jnp.minimum(b - ref, 0.0))
            k_t = (kk * jnp.exp(jnp.minimum(ref - b, 0.0))).astype(BF16)
            q_01 = jnp.concatenate([jnp.where(head0, q_t, 0.0), jnp.where(head0, 0.0, q_t)], axis=0)
            a01s[p] = a01s[p] + msk_ref[lev] * _dot_nt(q_01.astype(BF16), k_t)
    for p in pairs:
        av = jnp.dot(a01s[p].astype(BF16), vibs[p], preferred_element_type=F32)
        outs[p] = outs[p] + jnp.where(head0, av[:L], av[L:])
    for p in pairs:
        o = outs[p]
        gg = ph_ref[0, :, 3 * W_HGRN + p * HP:3 * W_HGRN + (p + 1) * HP]
        y = o * lax.rsqrt(head_sum(o * o) * (1.0 / HEAD_DIM) + EPS) * norm_ref[:, sls[p]]
        y = y * (gg * _sigmoid(gg))
        y_ref[0, :, sls[p]] = y.astype(y_ref.dtype)

    @pl.when(t == pl.num_programs(1) - 1)
    def _():
        s_ref[0] = st_s[...]


def _hgrn(ph, log_lb, log1m_lb, norm, s0, L):
    B, T, _ = ph.shape
    HP = 2 * HEAD_DIM
    NP = H_HGRN // 2
    sel, msk = _hgrn_levels(L)
    NL = msk.shape[0]
    s0t = jnp.swapaxes(s0, -1, -2).reshape(B, NP, 2, HEAD_DIM, HEAD_DIM)
    s0bd = jnp.einsum('bpavk,ac->bpavck', s0t, jnp.eye(2, dtype=F32)).reshape(B, NP, HP, HP)
    tok = lambda b, t: (b, t, 0)
    per_b4 = lambda b, t: (b, 0, 0, 0)
    const = lambda b, t: (0, 0)
    const3 = lambda b, t: (0, 0, 0)
    y, st = pl.pallas_call(
        functools.partial(_hgrn_kernel, L=L, NL=NL),
        grid=(B, T // L),
        in_specs=[pl.BlockSpec((1, L, 4 * W_HGRN), tok),
                  pl.BlockSpec((1, W_HGRN), const),
                  pl.BlockSpec((1, W_HGRN), const),
                  pl.BlockSpec((1, W_HGRN), const),
                  pl.BlockSpec((1, NP, HP, HP), per_b4),
                  pl.BlockSpec((NL * L, L), const),
                  pl.BlockSpec((NL, 2 * L, L), const3)],
        out_specs=[pl.BlockSpec((1, L, W_HGRN), tok),
                   pl.BlockSpec((1, NP, HP, HP), per_b4)],
        out_shape=[jax.ShapeDtypeStruct((B, T, W_HGRN), BF16),
                   jax.ShapeDtypeStruct((B, NP, HP, HP), F32)],
        scratch_shapes=[pltpu.VMEM((NP, HP, HP), F32)],
        compiler_params=_cparams(("arbitrary", "arbitrary")),
        name="hgrn2",
    )(ph, log_lb, log1m_lb, norm, s0bd, sel, msk)
    st6 = st.reshape(B, NP, 2, HEAD_DIM, 2, HEAD_DIM)
    st_heads = jnp.stack([st6[:, :, 0, :, 0, :], st6[:, :, 1, :, 1, :]], axis=2)
    return y, jnp.swapaxes(st_heads.reshape(B, H_HGRN, HEAD_DIM, HEAD_DIM), -1, -2)


def _topk_mask(gate, n_valid, lane, n_max):
    cnt = jnp.zeros(gate.shape, F32)
    for n in range(n_max):
        col = gate[:, n:n + 1]
        ahead = jnp.where(lane > n, jnp.where(col >= gate, 1.0, 0.0), jnp.where(col > gate, 1.0, 0.0))
        cnt = cnt + ahead * jnp.where(n < n_valid, 1.0, 0.0)
    return jnp.where(cnt < MOBA_TOPK, jnp.where(lane < n_valid, 1.0, 0.0), 0.0)


def _moba_prompt_kernel(q_ref, k_ref, v_ref, y_ref, km_s, kb_s, vt_s, s_s, *, NB):
    j = pl.program_id(2)
    BS = MOBA_BLOCK

    @pl.when(j == 0)
    def _():
        km_s[...] = jnp.zeros(km_s.shape, F32)
        for n in range(NB):
            kblk = k_ref[0, n * BS:(n + 1) * BS, :]
            km_s[n:n + 1, :] = jnp.mean(kblk, axis=0, keepdims=True)
            kb_s[n] = kblk.astype(BF16)
            vt_s[n] = jnp.transpose(v_ref[0, n * BS:(n + 1) * BS, :]).astype(BF16)

    sub = lax.broadcasted_iota(jnp.int32, (8, BS), 0)
    ki = lax.broadcasted_iota(jnp.int32, (BS, BS), 0)
    qi = lax.broadcasted_iota(jnp.int32, (BS, BS), 1)

    def tile(cap):
        sls = [slice(h * HEAD_DIM, (h + 1) * HEAD_DIM) for h in range(2)]
        sels, ms = [], []
        for h in range(2):
            sl = sls[h]
            q = q_ref[0, :, sl]
            gate = lax.dot_general(km_s[:, sl], q, (((1,), (1,)), ((), ())),
                                   precision=lax.Precision.HIGHEST, preferred_element_type=F32)
            cnt = jnp.zeros((8, BS), F32)
            for n in range(cap):
                row = gate[n:n + 1, :]
                ahead = jnp.where(sub > n, jnp.where(row >= gate, 1.0, 0.0), jnp.where(row > gate, 1.0, 0.0))
                cnt = cnt + ahead * jnp.where(n < j, 1.0, 0.0)
            sel = jnp.where(cnt < MOBA_TOPK, jnp.where(sub < j, 1.0, 0.0), 0.0)
            qb = (q * (HEAD_DIM ** -0.5 * math.log2(math.e))).astype(BF16)

            s = jnp.where(ki <= qi, _dot_nt(kb_s[j, :, sl], qb), NEG_BIG)
            s_s[h, cap] = s
            m = jnp.max(s, axis=0, keepdims=True)
            for n in range(cap):
                s = _dot_nt(kb_s[n, :, sl], qb)
                s_s[h, n] = s
                m = jnp.maximum(m, jnp.where(sel[n:n + 1, :] > 0.0, jnp.max(s, axis=0, keepdims=True), NEG_BIG))
            sels.append(sel)
            ms.append(m)
        outs = []
        for h in range(2):
            sl, sel, m = sls[h], sels[h], ms[h]
            l = jnp.zeros((1, BS), F32)
            acc = jnp.zeros((HEAD_DIM, BS), F32)
            for n in range(cap + 1):
                off = jnp.where(sel[n:n + 1, :] > 0.0, m, -NEG_BIG) if n < cap else m
                p = jnp.exp2(s_s[h, n] - off)
                l = l + jnp.sum(p, axis=0, keepdims=True)
                vt = vt_s[n, sl, :] if n < cap else vt_s[j, sl, :]
                acc = acc + jnp.dot(vt, p.astype(BF16), preferred_element_type=F32)
            outs.append(acc / l)
        y_ref[0] = jnp.transpose(jnp.concatenate(outs, axis=0)).astype(y_ref.dtype)

    caps = sorted({max(1, NB * i // 4) for i in range(1, 5)})
    lo = 0
    for cap in caps:
        pl.when(jnp.logical_and(j >= lo, j < cap))(functools.partial(tile, cap))
        lo = cap


def _moba_prompt(q, k, v):
    B, T, _ = q.shape
    NB = T // MOBA_BLOCK
    assert NB <= 8
    HP = 2 * HEAD_DIM
    return pl.pallas_call(
        functools.partial(_moba_prompt_kernel, NB=NB),
        grid=(B, W_MOBA // HP, NB),
        in_specs=[pl.BlockSpec((1, MOBA_BLOCK, HP), lambda b, h, j: (b, j, h)),
                  pl.BlockSpec((1, T, HP), lambda b, h, j: (b, 0, h)),
                  pl.BlockSpec((1, T, HP), lambda b, h, j: (b, 0, h))],
        out_specs=pl.BlockSpec((1, MOBA_BLOCK, HP), lambda b, h, j: (b, j, h)),
        out_shape=jax.ShapeDtypeStruct((B, T, W_MOBA), BF16),
        scratch_shapes=[pltpu.VMEM((8, HP), F32),
                        pltpu.VMEM((NB, MOBA_BLOCK, HP), BF16),
                        pltpu.VMEM((NB, HP, MOBA_BLOCK), BF16),
                        pltpu.VMEM((2, NB + 1, MOBA_BLOCK, MOBA_BLOCK), F32)],
        compiler_params=_cparams(("arbitrary", "arbitrary", "arbitrary")),
        name="moba_prompt",
    )(q, k, v)


PAGES_PER_STEP = 16


def _moba_sample_kernel(pt_ref, q_ref, kn_ref, vn_ref, ck_hbm, cv_hbm, y_ref,
                        kbuf, vbuf, sem, gate_s, m_s, l_s, o_s, *, layer, PPS):
    b = pl.program_id(0)
    c = pl.program_id(1)
    n_chunks = pl.num_programs(1)
    step = b * n_chunks + c
    total = pl.num_programs(0) * n_chunks
    slot = step % 2
    nb_per = PPS // 2
    R = q_ref.shape[1]

    def page_copies(bb, cc, sl):
        out = []
        for p in range(PPS):
            page = pt_ref[bb, cc * PPS + p]
            out.append(pltpu.make_async_copy(ck_hbm.at[page, layer], kbuf.at[sl, p], sem.at[0, sl]))
            out.append(pltpu.make_async_copy(cv_hbm.at[page, layer], vbuf.at[sl, p], sem.at[1, sl]))
        return out

    @pl.when(step == 0)
    def _():
        for cp in page_copies(b, c, slot):
            cp.start()

    @pl.when(step + 1 < total)
    def _():
        nxt = step + 1
        for cp in page_copies(nxt // n_chunks, nxt % n_chunks, 1 - slot):
            cp.start()

    for cp in page_copies(b, c, slot):
        cp.wait()

    @pl.when(c == 0)
    def _():
        gate_s[...] = jnp.zeros(gate_s.shape, F32)
        m_s[...] = jnp.zeros(m_s.shape, F32)
        l_s[...] = jnp.zeros(l_s.shape, F32)

    q_s = q_ref[0] * (HEAD_DIM ** -0.5)
    diag = (lax.broadcasted_iota(jnp.int32, (R, W_MOBA), 0) % H_MOBA
            == lax.broadcasted_iota(jnp.int32, (R, W_MOBA), 1) // HEAD_DIM)
    qbd = jnp.where(diag, jnp.concatenate([q_s] * H_MOBA, axis=1), 0.0)
    q_hi = qbd.astype(BF16)
    q_lo = (qbd - q_hi.astype(F32)).astype(BF16)
    q2 = jnp.concatenate([q_hi, q_lo], axis=0)
    lane = lax.broadcasted_iota(jnp.int32, (R, 128), 1)
    half = MOBA_BLOCK // 2

    gate, m_run, l_run = gate_s[...], m_s[...], l_s[...]
    s_pages = []
    for pg in range(PPS):
        kt = kbuf[slot, pg].reshape(W_MOBA, PAGE_SIZE).astype(BF16)
        s2 = jnp.dot(q2, kt, preferred_element_type=F32)
        s_pages.append(s2[:R] + s2[R:])
    probs = []
    for i in range(nb_per):
        s = jnp.concatenate(s_pages[2 * i:2 * i + 2], axis=1)
        gcol = jnp.sum(s, axis=-1, keepdims=True)
        mcol = jnp.max(s, axis=-1, keepdims=True)
        p = jnp.exp(s - mcol)
        lcol = jnp.sum(p, axis=-1, keepdims=True)
        probs.append(p.astype(BF16))
        hit = lane == c * nb_per + i
        gate = jnp.where(hit, gcol, gate)
        m_run = jnp.where(hit, mcol, m_run)
        l_run = jnp.where(hit, lcol, l_run)
    for i in range(nb_per):
        pb = probs[i]
        o_s[c * nb_per + i] = (
            _dot_nt(pb[:, :half], vbuf[slot, 2 * i].reshape(W_MOBA, PAGE_SIZE).astype(BF16))
            + _dot_nt(pb[:, half:], vbuf[slot, 2 * i + 1].reshape(W_MOBA, PAGE_SIZE).astype(BF16)))
    gate_s[...] = gate
    m_s[...] = m_run
    l_s[...] = l_run

    @pl.when(c == n_chunks - 1)
    def _():
        n_blk = o_s.shape[0]
        sel = _topk_mask(gate_s[...], n_blk, lane, n_blk) > 0.0
        ri = lax.broadcasted_iota(jnp.int32, (R, R), 0)
        ci = lax.broadcasted_iota(jnp.int32, (R, R), 1)
        own_ok = jnp.logical_and(ri % H_MOBA == ci % H_MOBA, ci // H_MOBA <= ri // H_MOBA)
        s_own = lax.dot_general(q_s, kn_ref[0], (((1,), (1,)), ((), ())),
                                precision=lax.Precision.HIGHEST, preferred_element_type=F32)
        s_own = jnp.where(own_ok, s_own, NEG_BIG)
        m_tot = jnp.maximum(jnp.max(jnp.where(sel, m_s[...], NEG_BIG), axis=-1, keepdims=True),
                            jnp.max(s_own, axis=-1, keepdims=True))
        w = jnp.where(sel, jnp.exp(m_s[...] - m_tot), 0.0)
        p_own = jnp.exp(s_own - m_tot)
        den = jnp.sum(w * l_s[...], axis=-1, keepdims=True) + jnp.sum(p_own, axis=-1, keepdims=True)
        wide = jnp.zeros((R, W_MOBA), F32)
        for n in range(n_blk):
            wide = wide + w[:, n:n + 1] * o_s[n]
        wide = jnp.where(diag, wide, 0.0)
        acc = lax.dot_general(p_own, vn_ref[0], (((1,), (0,)), ((), ())),
                              precision=lax.Precision.HIGHEST, preferred_element_type=F32)
        for h in range(H_MOBA):
            acc = acc + wide[:, h * HEAD_DIM:(h + 1) * HEAD_DIM]
        y_ref[0] = (acc / den).astype(y_ref.dtype)


def _moba_sample(q, kn, vn, cache_k, cache_v, page_table, layer):
    B, R, _ = q.shape
    n_pages = page_table.shape[1]
    PPS = min(PAGES_PER_STEP, n_pages)
    n_chunks = n_pages // PPS
    n_blk = n_pages * PAGE_SIZE // MOBA_BLOCK
    assert n_blk <= 128
    per_b = lambda b, c, pt: (b, 0, 0)
    grid_spec = pltpu.PrefetchScalarGridSpec(
        num_scalar_prefetch=1,
        grid=(B, n_chunks),
        in_specs=[pl.BlockSpec((1, R, HEAD_DIM), per_b),
                  pl.BlockSpec((1, R, HEAD_DIM), per_b),
                  pl.BlockSpec((1, R, HEAD_DIM), per_b),
                  pl.BlockSpec(memory_space=pl.ANY),
                  pl.BlockSpec(memory_space=pl.ANY)],
        out_specs=pl.BlockSpec((1, R, HEAD_DIM), per_b),
        scratch_shapes=[pltpu.VMEM((2, PPS, H_MOBA, HEAD_DIM, PAGE_SIZE), F32),
                        pltpu.VMEM((2, PPS, H_MOBA, HEAD_DIM, PAGE_SIZE), F32),
                        pltpu.SemaphoreType.DMA((2, 2)),
                        pltpu.VMEM((R, 128), F32),
                        pltpu.VMEM((R, 128), F32),
                        pltpu.VMEM((R, 128), F32),
                        pltpu.VMEM((n_blk, R, W_MOBA), F32)])
    return pl.pallas_call(
        functools.partial(_moba_sample_kernel, layer=layer, PPS=PPS),
        grid_spec=grid_spec,
        out_shape=jax.ShapeDtypeStruct((B, R, HEAD_DIM), BF16),
        compiler_params=_cparams(("arbitrary", "arbitrary")),
        name="moba_sample",
    )(page_table, q, kn, vn, cache_k, cache_v)


def _outproj_kernel(x_ref, ya_ref, yb_ref, yc_ref, w_ref, g_ref, o_ref):
    a0, a1 = W_MLSTM, W_MLSTM + W_MOBA
    mix = (jnp.dot(ya_ref[...], w_ref[0:a0, :], preferred_element_type=F32)
           + jnp.dot(yb_ref[...], w_ref[a0:a1, :], preferred_element_type=F32)
           + jnp.dot(yc_ref[...], w_ref[a1:, :], preferred_element_type=F32))
    o_ref[...] = x_ref[...] + mix * lax.rsqrt(jnp.mean(mix * mix, axis=-1, keepdims=True) + EPS) * g_ref[...]


def _outproj(x2d, ya, yb, yc, w, g, tm):
    n = x2d.shape[0]
    row = lambda i: (i, 0)
    const = lambda i: (0, 0)
    return pl.pallas_call(
        _outproj_kernel,
        grid=(n // tm,),
        in_specs=[pl.BlockSpec((tm, D_MODEL), row),
                  pl.BlockSpec((tm, W_MLSTM), row),
                  pl.BlockSpec((tm, W_MOBA), row),
                  pl.BlockSpec((tm, W_HGRN), row),
                  pl.BlockSpec((D_MODEL, D_MODEL), const),
                  pl.BlockSpec((1, D_MODEL), const)],
        out_specs=pl.BlockSpec((tm, D_MODEL), row),
        out_shape=jax.ShapeDtypeStruct((n, D_MODEL), F32),
        compiler_params=_cparams(("arbitrary",)),
        name="outproj",
    )(x2d, ya, yb, yc, w, g)


FF_CHUNKS = 2
FF_CHUNK = D_FF // FF_CHUNKS


def _gelu_tanh(x):
    return 0.5 * x * (1.0 + jnp.tanh(math.sqrt(2.0 / math.pi) * (x + 0.044715 * (x * x * x))))


def _ffn_kernel(x_ref, g2_ref, wu_ref, wg_ref, wd_ref, cw_ref, cb_ref, c0_ref, g3_ref,
                o_ref, tail_ref, hb_s, acc_s, ext_s, *, tm, R, S):
    tt = pl.program_id(1)
    c = pl.program_id(2)

    @pl.when(c == 0)
    def _():
        x = x_ref[...]
        h = x * lax.rsqrt(jnp.mean(x * x, axis=-1, keepdims=True) + EPS) * g2_ref[...]
        hb_s[...] = h.astype(BF16)

    @pl.when(tt == 0)
    def _():
        ext_s[c, 0:R, :] = c0_ref[0]

    hb = hb_s[...]
    u = jnp.dot(hb, wu_ref[...], preferred_element_type=F32)
    g = jnp.dot(hb, wg_ref[...], preferred_element_type=F32)
    ext_s[c, R:R + tm, :] = u
    uc = (cb_ref[...]
          + cw_ref[0:1, :] * ext_s[c, R - 2 * S:R - 2 * S + tm, :]
          + cw_ref[1:2, :] * ext_s[c, R - S:R - S + tm, :]
          + cw_ref[2:3, :] * u)
    act = (_gelu_tanh(uc) * g).astype(BF16)
    d = jnp.dot(act, wd_ref[...], preferred_element_type=F32)

    @pl.when(c == 0)
    def _():
        acc_s[...] = d

    @pl.when(c > 0)
    def _():
        acc_s[...] += d

    tail = ext_s[c, tm:tm + R, :]
    tail_ref[0, c] = tail
    ext_s[c, 0:R, :] = tail

    @pl.when(c == pl.num_programs(2) - 1)
    def _():
        a = acc_s[...]
        o_ref[...] = x_ref[...] + a * lax.rsqrt(jnp.mean(a * a, axis=-1, keepdims=True) + EPS) * g3_ref[...]


def _ffn(x2d, g2, w_up, w_down, conv_w8, conv_b, carry0, g3, *, nseq, tm, R, S):
    n = x2d.shape[0]
    ntt = n // (nseq * tm)
    row = lambda s, t, c: (s * ntt + t, 0)
    const = lambda s, t, c: (0, 0)
    x_new, tail = pl.pallas_call(
        functools.partial(_ffn_kernel, tm=tm, R=R, S=S),
        grid=(nseq, ntt, FF_CHUNKS),
        in_specs=[pl.BlockSpec((tm, D_MODEL), row),
                  pl.BlockSpec((1, D_MODEL), const),
                  pl.BlockSpec((D_MODEL, FF_CHUNK), lambda s, t, c: (0, c)),
                  pl.BlockSpec((D_MODEL, FF_CHUNK), lambda s, t, c: (0, FF_CHUNKS + c)),
                  pl.BlockSpec((FF_CHUNK, D_MODEL), lambda s, t, c: (c, 0)),
                  pl.BlockSpec((8, FF_CHUNK), lambda s, t, c: (0, c)),
                  pl.BlockSpec((1, FF_CHUNK), lambda s, t, c: (0, c)),
                  pl.BlockSpec((1, R, FF_CHUNK), lambda s, t, c: (s, 0, c)),
                  pl.BlockSpec((1, D_MODEL), const)],
        out_specs=[pl.BlockSpec((tm, D_MODEL), row),
                   pl.BlockSpec((1, FF_CHUNKS, R, FF_CHUNK), lambda s, t, c: (s, 0, 0, 0))],
        out_shape=[jax.ShapeDtypeStruct((n, D_MODEL), F32),
                   jax.ShapeDtypeStruct((nseq, FF_CHUNKS, R, FF_CHUNK), F32)],
        scratch_shapes=[pltpu.VMEM((tm, D_MODEL), BF16),
                        pltpu.VMEM((tm, D_MODEL), F32),
                        pltpu.VMEM((FF_CHUNKS, R + tm, FF_CHUNK), F32)],
        compiler_params=_cparams(("arbitrary", "arbitrary", "arbitrary")),
        name="convffn",
    )(x2d, g2, w_up, w_up, w_down, conv_w8, conv_b, carry0, g3)
    return x_new, jnp.transpose(tail, (0, 2, 1, 3)).reshape(nseq, R, D_FF)


def _rope_tables(pos):
    half = HEAD_DIM // 2
    inv = ROPE_THETA ** (-jnp.arange(half, dtype=F32) / half)
    ang = pos.astype(F32)[:, None] * inv[None, :]
    cos = jnp.cos(ang)
    sin = jnp.sin(ang)
    cos_t = jnp.tile(jnp.concatenate([cos, cos], axis=-1), (1, H_MOBA))
    sin_t = jnp.tile(jnp.concatenate([-sin, sin], axis=-1), (1, H_MOBA))
    return cos_t, sin_t


def _layer_weights(l, w_in_r, gate_bias, mlstm_norm, lower_bounds, hgrn_norm, w_out_b, norms,
                   w_up_b, conv_w, conv_b, w_down_b):
    bias = gate_bias[l].reshape(2 * H_MLSTM)
    lb = lower_bounds[l]
    return dict(
        w_in=w_in_r[l], w_out=w_out_b[l], w_up=w_up_b[l], w_down=w_down_b[l],
        g0=norms[l, 0][None], g1=norms[l, 1][None], g2=norms[l, 2][None], g3=norms[l, 3][None],
        bias_row=jnp.pad(bias, (0, GATE_W - 2 * H_MLSTM))[None],
        bias_col=bias[:, None],
        m_norm=mlstm_norm[l][None], h_norm=hgrn_norm[l][None],
        log_lb=jnp.log(lb)[None], log1m_lb=jnp.log1p(-lb)[None],
        conv_w8=jnp.pad(conv_w[l], ((0, 8 - CONV_W), (0, 0))), conv_b=conv_b[l][None])


def _layer_prompt(x2d, B, T, lw, cos, sin):
    pm, q, k, v, ph, gt = _inproj(x2d, lw["g0"], lw["w_in"], cos, sin, 256)
    r3 = lambda a: a.reshape(B, T, a.shape[-1])
    gates = r3(gt)
    gates_t = jnp.transpose(gates[:, :, :2 * H_MLSTM], (0, 2, 1))
    c0 = jnp.zeros((B, H_MLSTM, HEAD_DIM, HEAD_DIM), F32)
    n0 = jnp.zeros((B, H_MLSTM, HEAD_DIM), F32)
    m0 = jnp.zeros((B, 1, GATE_W), F32)
    ya, c_new, n_new, m_new = _mlstm(r3(pm), gates, gates_t, lw["bias_row"], lw["bias_col"],
                                     lw["m_norm"], c0, n0, m0, min(128, T))
    yb = _moba_prompt(r3(q), r3(k), r3(v))
    s0 = jnp.zeros((B, H_HGRN, HEAD_DIM, HEAD_DIM), F32)
    yc, s_new = _hgrn(r3(ph), lw["log_lb"], lw["log1m_lb"], lw["h_norm"], s0, min(128, T))
    r2 = lambda a: a.reshape(B * T, a.shape[-1])
    x1 = _outproj(x2d, r2(ya), r2(yb), r2(yc), lw["w_out"], lw["g1"], 512)
    carry0 = jnp.zeros((B, 8, D_FF), F32)
    x2, tail = _ffn(x1, lw["g2"], lw["w_up"], lw["w_down"], lw["conv_w8"], lw["conv_b"], carry0,
                    lw["g3"], nseq=B, tm=min(512, T), R=8, S=1)
    return (x2, k, v, c_new, n_new, m_new[:, 0, :H_MLSTM], s_new, tail[:, 8 - (CONV_W - 1):])


SAMPLE_CHUNK = 8


def _layer_sample(x2d, B, T, lw, cos, sin, cache_k, cache_v, page_table, layer, c0, n0, m0, s0, conv0):
    pm, q, k, v, ph, gt = _inproj(x2d, lw["g0"], lw["w_in"], cos, sin, B * T)
    L = SAMPLE_CHUNK

    def seq_major(a, pad_row=None):
        a = jnp.transpose(a.reshape(T, B, a.shape[-1]), (1, 0, 2))
        if pad_row is None:
            return a
        fill = jnp.broadcast_to(pad_row[None, None, :], (B, L - T, a.shape[-1]))
        return jnp.concatenate([a, fill], axis=1)

    def time_major(a):
        return jnp.transpose(a[:, :T], (1, 0, 2)).reshape(T * B, a.shape[-1])

    gate_pad = jnp.where(jnp.arange(GATE_W) < H_MLSTM, NEG_BIG, -NEG_BIG).astype(F32)
    gates = seq_major(gt, gate_pad)
    gates_t = jnp.transpose(gates[:, :, :2 * H_MLSTM], (0, 2, 1))
    m0p = jnp.pad(m0, ((0, 0), (0, GATE_W - H_MLSTM)))[:, None, :]
    ya, c_new, n_new, m_new = _mlstm(seq_major(pm, jnp.zeros((4 * W_MLSTM,), F32)), gates, gates_t,
                                     lw["bias_row"], lw["bias_col"], lw["m_norm"], c0, n0, m0p, L)
    col = jnp.arange(4 * W_HGRN)
    h_pad = jnp.where((col >= W_HGRN) & (col < 2 * W_HGRN), -NEG_BIG, 0.0).astype(F32)
    yc, s_new = _hgrn(seq_major(ph, h_pad), lw["log_lb"], lw["log1m_lb"], lw["h_norm"], s0, L)
    heads = lambda a: seq_major(a).reshape(B, T * H_MOBA, HEAD_DIM)
    yb = _moba_sample(heads(q), heads(k), heads(v), cache_k, cache_v, page_table, layer)
    yb = yb.reshape(B, T, W_MOBA)
    x1 = _outproj(x2d, time_major(ya), time_major(yb), time_major(yc), lw["w_out"], lw["g1"], B * T)
    carry0 = jnp.transpose(conv0, (1, 0, 2)).reshape(1, (CONV_W - 1) * B, D_FF)
    x2, tail = _ffn(x1, lw["g2"], lw["w_up"], lw["w_down"], lw["conv_w8"], lw["conv_b"], carry0,
                    lw["g3"], nseq=1, tm=B * T, R=(CONV_W - 1) * B, S=B)
    conv_new = jnp.transpose(tail.reshape(CONV_W - 1, B, D_FF), (1, 0, 2))
    tb = lambda a: jnp.transpose(a.reshape(T, B, H_MOBA, HEAD_DIM), (1, 0, 2, 3))
    return (x2, tb(k), tb(v), c_new, n_new, m_new[:, 0, :H_MLSTM], s_new, conv_new)


def kernel(x_prompt, x_sample, cache_k, cache_v, page_table, state_mlstm_c, state_mlstm_n,
           state_mlstm_m, state_hgrn, state_conv, w_in, mlstm_gate_bias, mlstm_norm,
           hgrn_lb_logits, hgrn_norm, w_out, norms, ffn_w_up, ffn_conv_w, ffn_conv_b, ffn_w_down):
    Bp, Tp, _ = x_prompt.shape
    Bs, Ts, _ = x_sample.shape
    depth = w_in.shape[0]
    past_len = page_table.shape[1] * PAGE_SIZE

    lb_cs = jnp.cumsum(jax.nn.softmax(hgrn_lb_logits.astype(F32), axis=0), axis=0)
    lower_bounds = lb_cs - lb_cs[0:1]
    g0 = 4 * W_MLSTM
    g1 = g0 + 2 * H_MLSTM
    w_in_r = jnp.concatenate(
        [w_in[:, :, :g0], w_in[:, :, g1:], w_in[:, :, g0:g1],
         jnp.zeros((depth, D_MODEL, GATE_W - 2 * H_MLSTM), w_in.dtype)], axis=-1).astype(BF16)
    w_out_b = w_out.astype(BF16)
    w_up_b = ffn_w_up.astype(BF16)
    w_down_b = ffn_w_down.astype(BF16)
    cos_p, sin_p = _rope_tables(jnp.arange(Tp))
    cos_s, sin_s = _rope_tables(past_len + jnp.arange(Ts * Bs) // Bs)

    cache_k = jnp.transpose(cache_k, (0, 1, 3, 4, 2))
    cache_v = jnp.transpose(cache_v, (0, 1, 3, 4, 2))

    xp = x_prompt.reshape(Bp * Tp, D_MODEL)
    xs = jnp.transpose(x_sample, (1, 0, 2)).reshape(Ts * Bs, D_MODEL)
    rec_p, rec_s = [], []
    for l in range(depth):
        lw = _layer_weights(l, w_in_r, mlstm_gate_bias, mlstm_norm, lower_bounds, hgrn_norm,
                            w_out_b, norms, w_up_b, ffn_conv_w, ffn_conv_b, w_down_b)
        xp, *new_p = _layer_prompt(xp, Bp, Tp, lw, cos_p, sin_p)
        xs, *new_s = _layer_sample(xs, Bs, Ts, lw, cos_s, sin_s, cache_k, cache_v, page_table, l,
                                   state_mlstm_c[l], state_mlstm_n[l], state_mlstm_m[l],
                                   state_hgrn[l], state_conv[l])
        rec_p.append(new_p)
        rec_s.append(new_s)

    def stack(rec, i, axis):
        return jnp.stack([r[i] for r in rec], axis=axis)

    n_pg = Tp // PAGE_SIZE
    page = lambda a: a.reshape(Bp, n_pg, PAGE_SIZE, H_MOBA, HEAD_DIM)
    k_prompt = jnp.stack([page(r[0]) for r in rec_p], axis=2)
    v_prompt = jnp.stack([page(r[1]) for r in rec_p], axis=2)
    y_prompt = xp.reshape(Bp, Tp, D_MODEL)
    y_sample = jnp.transpose(xs.reshape(Ts, Bs, D_MODEL), (1, 0, 2))
    return (y_prompt, y_sample, k_prompt, v_prompt,
            stack(rec_p, 2, 0), stack(rec_p, 3, 0), stack(rec_p, 4, 0), stack(rec_p, 5, 0),
            stack(rec_p, 6, 0),
            stack(rec_s, 0, 1), stack(rec_s, 1, 1),
            stack(rec_s, 2, 0), stack(rec_s, 3, 0), stack(rec_s, 4, 0), stack(rec_s, 5, 0),
            stack(rec_s, 6, 0))
```

```python
import functools
import math

import jax
import numpy as np
import jax.numpy as jnp
from jax import lax
from jax.experimental import pallas as pl
from jax.experimental.pallas import tpu as pltpu

F32 = jnp.float32
BF16 = jnp.bfloat16

D_MODEL = 1024
HEAD_DIM = 64
H_MLSTM = 4
H_MOBA = 8
H_HGRN = 4
W_MLSTM = H_MLSTM * HEAD_DIM
W_MOBA = H_MOBA * HEAD_DIM
W_HGRN = H_HGRN * HEAD_DIM
MOBA_BLOCK = 256
MOBA_TOPK = 3
ROPE_THETA = 10000.0
D_FF = 2816
CONV_W = 3
EPS = 1e-6
PAGE_SIZE = 128
GATE_W = 128
PROJ_W = 4 * W_MLSTM + 3 * W_MOBA + 4 * W_HGRN + GATE_W
NEG_BIG = -1e30
VMEM_LIMIT = 56 * 1024 * 1024


def _cparams(sem):
    return pltpu.CompilerParams(dimension_semantics=sem, vmem_limit_bytes=VMEM_LIMIT)


def _log_sigmoid(x):
    return jnp.minimum(x, 0.0) - jnp.log1p(jnp.exp(-jnp.abs(x)))


def _sigmoid(x):
    return 1.0 / (1.0 + jnp.exp(-x))


def _split3(x):
    h1 = x.astype(BF16)
    r1 = x - h1.astype(F32)
    h2 = r1.astype(BF16)
    h3 = (r1 - h2.astype(F32)).astype(BF16)
    return h1, h2, h3


def _cumsum_rows(tri, x):
    return sum(jnp.dot(tri, p, preferred_element_type=F32) for p in _split3(x))


def _cumsum_lanes(x, tri_u):
    return sum(jnp.dot(p, tri_u, preferred_element_type=F32) for p in _split3(x))


def _dot_nt(a, b):
    return lax.dot_general(a, b, (((1,), (1,)), ((), ())), preferred_element_type=F32)


def _dot_tn(a, b):
    return lax.dot_general(a, b, (((0,), (0,)), ((), ())), preferred_element_type=F32)


def _inproj_kernel(x_ref, g_ref, w_ref, cos_ref, sin_ref, *rest, paged):
    if paged:
        _, _, pm_ref, q_ref, k_ref, v_ref, ph_ref, gt_ref, kt_ref, vt_ref = rest
    else:
        pm_ref, q_ref, k_ref, v_ref, ph_ref, gt_ref = rest
    x = x_ref[...]
    h = x * lax.rsqrt(jnp.mean(x * x, axis=-1, keepdims=True) + EPS) * g_ref[...]
    p = jnp.dot(h.astype(BF16), w_ref[...], preferred_element_type=F32)
    o = 4 * W_MLSTM
    pm_ref[...] = p[:, 0:o]
    cos = cos_ref[...]
    sin = sin_ref[...]
    lane = lax.broadcasted_iota(jnp.int32, cos.shape, 1)
    first_half = (lane % HEAD_DIM) < (HEAD_DIM // 2)

    def rope(a):
        swapped = jnp.where(first_half,
                            pltpu.roll(a, W_MOBA - HEAD_DIM // 2, axis=1),
                            pltpu.roll(a, HEAD_DIM // 2, axis=1))
        return a * cos + swapped * sin

    q_ref[...] = rope(p[:, o:o + W_MOBA])
    k_rot = rope(p[:, o + W_MOBA:o + 2 * W_MOBA])
    v = p[:, o + 2 * W_MOBA:o + 3 * W_MOBA]
    k_ref[...] = k_rot
    v_ref[...] = v
    o2 = o + 3 * W_MOBA
    ph_ref[...] = p[:, o2:o2 + 4 * W_HGRN]
    gt_ref[...] = p[:, o2 + 4 * W_HGRN:]
    if paged:
        for src, dst in ((k_rot, kt_ref), (v, vt_ref)):
            src_t = jnp.transpose(src)
            for pg in range(src.shape[0] // PAGE_SIZE):
                dst[pg, 0] = src_t[:, pg * PAGE_SIZE:(pg + 1) * PAGE_SIZE].reshape(H_MOBA, HEAD_DIM, PAGE_SIZE)


def _inproj(x2d, g, w, cos, sin, tm, pages=None):
    n = x2d.shape[0]
    tab_blocks = cos.shape[0] // tm
    row = lambda i: (i, 0)
    const = lambda i: (0, 0)
    tab = lambda i: (i % tab_blocks, 0)
    widths = (4 * W_MLSTM, W_MOBA, W_MOBA, W_MOBA, 4 * W_HGRN, GATE_W)
    in_specs = [pl.BlockSpec((tm, D_MODEL), row),
                pl.BlockSpec((1, D_MODEL), const),
                pl.BlockSpec((D_MODEL, PROJ_W), const),
                pl.BlockSpec((tm, W_MOBA), tab),
                pl.BlockSpec((tm, W_MOBA), tab)]
    out_specs = [pl.BlockSpec((tm, wd), row) for wd in widths]
    out_shape = [jax.ShapeDtypeStruct((n, wd), F32) for wd in widths]
    args = [x2d, g, w, cos, sin]
    aliases = {}
    if pages is not None:
        k_pages, v_pages, layer = pages
        page_blk = pl.BlockSpec((tm // PAGE_SIZE, 1, H_MOBA, HEAD_DIM, PAGE_SIZE),
                                lambda i: (i, layer, 0, 0, 0))
        in_specs += [pl.BlockSpec(memory_space=pl.ANY)] * 2
        out_specs += [page_blk, page_blk]
        out_shape += [jax.ShapeDtypeStruct(k_pages.shape, F32), jax.ShapeDtypeStruct(v_pages.shape, F32)]
        args += [k_pages, v_pages]
        aliases = {5: 6, 6: 7}
    return pl.pallas_call(
        functools.partial(_inproj_kernel, paged=pages is not None),
        grid=(n // tm,),
        in_specs=in_specs,
        out_specs=out_specs,
        out_shape=out_shape,
        input_output_aliases=aliases,
        compiler_params=_cparams(("arbitrary",)),
        name="inproj",
    )(*args)


def _mlstm_t_kernel(pm_ref, gc_ref, gr_ref, brow_ref, bcol_ref, norm_ref, c0_ref, n0_ref, m0_ref,
                    y_ref, c_ref, n_ref, m_ref, c_s, n_s, m_s, *, L):
    t = pl.program_id(1)

    @pl.when(t == 0)
    def _():
        c_s[...] = c0_ref[0]
        n_s[...] = n0_ref[0]
        m_s[...] = m0_ref[0]

    r = lax.broadcasted_iota(jnp.int32, (L, L), 0)
    c = lax.broadcasted_iota(jnp.int32, (L, L), 1)
    key_visible = r <= c
    tri_l = jnp.where(r >= c, 1.0, 0.0).astype(BF16)
    tri_u = jnp.where(r <= c, 1.0, 0.0).astype(BF16)

    pre_c = gc_ref[0] + brow_ref[...]
    b_c = _cumsum_rows(tri_l, _log_sigmoid(pre_c))
    pre_r = gr_ref[0] + bcol_ref[...]
    b_r = _cumsum_lanes(_log_sigmoid(pre_r), tri_u)
    lane_g = lax.broadcasted_iota(jnp.int32, (L, GATE_W), 1)
    z_parts = _split3(jnp.where(lane_g < H_MLSTM, pre_c, -b_c))
    e_row = lax.broadcasted_iota(jnp.int32, (GATE_W, L), 0)

    scale = HEAD_DIM ** -0.5
    v_t = jnp.transpose(pm_ref[0, :, 2 * W_MLSTM:3 * W_MLSTM])
    og_t = jnp.transpose(pm_ref[0, :, 3 * W_MLSTM:4 * W_MLSTM])
    outs = []
    for h in range(H_MLSTM):
        sl = slice(h * HEAD_DIM, (h + 1) * HEAD_DIM)
        q = pm_ref[0, :, sl]
        k = pm_ref[0, :, W_MLSTM + h * HEAD_DIM:W_MLSTM + (h + 1) * HEAD_DIM] * scale
        vt = v_t[sl]
        i_row = pre_r[h:h + 1, :]
        b_row = b_r[H_MLSTM + h:H_MLSTM + h + 1, :]
        m_prev = m_s[:, h:h + 1]
        c_t = c_s[h]
        n_row = n_s[h:h + 1, :]
        qb = q.astype(BF16)
        kb = k.astype(BF16)

        pick = jnp.where(jnp.logical_or(e_row == h, e_row == H_MLSTM + h), 1.0, 0.0).astype(BF16)
        u_rep = sum(jnp.dot(part, pick, preferred_element_type=F32) for part in z_parts)
        log_d = jnp.where(key_visible, u_rep + b_row, -jnp.inf)
        m_inter = b_row + m_prev
        m_t = jnp.maximum(m_inter, jnp.max(log_d, axis=0, keepdims=True))
        s = _dot_nt(kb, qb) * jnp.exp(log_d - m_t)
        g = jnp.exp(m_inter - m_t)
        num = g * _dot_nt(c_t.astype(BF16), qb) \
            + jnp.dot(vt.astype(BF16), s.astype(BF16), preferred_element_type=F32)
        qn = lax.dot_general(jnp.broadcast_to(n_row, (8, HEAD_DIM)), q, (((1,), (1,)), ((), ())),
                             precision=lax.Precision.HIGHEST, preferred_element_type=F32)[0:1]
        den = g * qn + jnp.sum(s, axis=0, keepdims=True)
        hh = num / jnp.maximum(jnp.abs(den), jnp.exp(-m_t))
        z = hh * _sigmoid(og_t[sl])
        outs.append(z * lax.rsqrt(jnp.mean(z * z, axis=0, keepdims=True) + EPS))

        b_last = b_row[:, L - 1:L]
        log_w = b_last - b_row + i_row
        m_new = jnp.maximum(b_last + m_prev, jnp.max(log_w, axis=-1, keepdims=True))
        decay = jnp.exp(b_last + m_prev - m_new)
        w_row = jnp.exp(log_w - m_new)
        c_s[h] = decay * c_t + jnp.dot((vt * w_row).astype(BF16), kb, preferred_element_type=F32)
        kw_sum = lax.dot_general(jnp.broadcast_to(w_row, (8, L)), k, (((1,), (0,)), ((), ())),
                                 precision=lax.Precision.HIGHEST, preferred_element_type=F32)[0:1]
        n_s[h:h + 1, :] = decay * n_row + kw_sum
        m_s[:, h:h + 1] = m_new

    y = jnp.transpose(jnp.concatenate(outs, axis=0)) * norm_ref[...]
    y_ref[0] = y.astype(y_ref.dtype)

    @pl.when(t == pl.num_programs(1) - 1)
    def _():
        c_ref[0] = c_s[...]
        n_ref[0] = n_s[...]
        m_ref[0] = m_s[...]


def _mlstm(pm, gates, gates_t, bias_row, bias_col, norm, c0, n0, m0, L):
    B, T, _ = pm.shape
    tok = lambda b, t: (b, t, 0)
    per_b3 = lambda b, t: (b, 0, 0)
    per_b4 = lambda b, t: (b, 0, 0, 0)
    const = lambda b, t: (0, 0)
    y, c_t, n_new, m_new = pl.pallas_call(
        functools.partial(_mlstm_t_kernel, L=L),
        grid=(B, T // L),
        in_specs=[pl.BlockSpec((1, L, 4 * W_MLSTM), tok),
                  pl.BlockSpec((1, L, GATE_W), tok),
                  pl.BlockSpec((1, 8, L), lambda b, t: (b, 0, t)),
                  pl.BlockSpec((1, GATE_W), const),
                  pl.BlockSpec((8, 1), const),
                  pl.BlockSpec((1, W_MLSTM), const),
                  pl.BlockSpec((1, H_MLSTM, HEAD_DIM, HEAD_DIM), per_b4),
                  pl.BlockSpec((1, H_MLSTM, HEAD_DIM), per_b3),
                  pl.BlockSpec((1, 1, GATE_W), per_b3)],
        out_specs=[pl.BlockSpec((1, L, W_MLSTM), tok),
                   pl.BlockSpec((1, H_MLSTM, HEAD_DIM, HEAD_DIM), per_b4),
                   pl.BlockSpec((1, H_MLSTM, HEAD_DIM), per_b3),
                   pl.BlockSpec((1, 1, GATE_W), per_b3)],
        out_shape=[jax.ShapeDtypeStruct((B, T, W_MLSTM), BF16),
                   jax.ShapeDtypeStruct((B, H_MLSTM, HEAD_DIM, HEAD_DIM), F32),
                   jax.ShapeDtypeStruct((B, H_MLSTM, HEAD_DIM), F32),
                   jax.ShapeDtypeStruct((B, 1, GATE_W), F32)],
        scratch_shapes=[pltpu.VMEM((H_MLSTM, HEAD_DIM, HEAD_DIM), F32),
                        pltpu.VMEM((H_MLSTM, HEAD_DIM), F32),
                        pltpu.VMEM((1, GATE_W), F32)],
        compiler_params=_cparams(("arbitrary", "arbitrary")),
        name="mlstm",
    )(pm, gates, gates_t, bias_row, bias_col, norm, jnp.swapaxes(c0, -1, -2), n0, m0)
    return y, jnp.swapaxes(c_t, -1, -2), n_new, m_new


def _hgrn_levels(L):
    n_lev = int(math.log2(L))
    assert 1 << n_lev == L
    r = np.arange(L)
    sel = np.zeros((n_lev, L, L), np.float32)
    msk = np.zeros((n_lev, L, L), np.float32)
    for lev in range(n_lev):
        w = 1 << lev
        mid = (r // (2 * w)) * (2 * w) + w
        sel[lev, r, mid - 1] = 1.0
        same = (r[:, None] // (2 * w)) == (r[None, :] // (2 * w))
        msk[lev] = same & ((r[:, None] % (2 * w)) >= w) & ((r[None, :] % (2 * w)) < w)
    return (jnp.asarray(sel.reshape(n_lev * L, L), BF16),
            jnp.asarray(np.concatenate([msk, msk], axis=1), F32))


def _hgrn_kernel(ph_ref, llb_ref, l1m_ref, norm_ref, s0_ref, sel_ref, msk_ref, y_ref, s_ref, st_s,
                 *, L, NL):
    t = pl.program_id(1)
    HP = 2 * HEAD_DIM

    @pl.when(t == 0)
    def _():
        st_s[...] = s0_ref[0]

    r = lax.broadcasted_iota(jnp.int32, (L, L), 0)
    c = lax.broadcasted_iota(jnp.int32, (L, L), 1)
    tri_l = jnp.where(r >= c, 1.0, 0.0).astype(BF16)
    head0 = lax.broadcasted_iota(jnp.int32, (L, HP), 1) < HEAD_DIM
    br = lax.broadcasted_iota(jnp.int32, (HP, HP), 0) // HEAD_DIM
    bc = lax.broadcasted_iota(jnp.int32, (HP, HP), 1) // HEAD_DIM
    blk = jnp.where(br == bc, 1.0, 0.0)
    blk_b = blk.astype(BF16)

    def head_sum(x):
        hi = x.astype(BF16)
        lo = (x - hi.astype(F32)).astype(BF16)
        both = jnp.dot(jnp.concatenate([hi, lo], axis=0), blk_b, preferred_element_type=F32)
        return both[:L] + both[L:]

    pairs = range(H_HGRN // 2)
    sls = [slice(p * HP, (p + 1) * HP) for p in pairs]
    qs = [ph_ref[0, :, sls[p]] for p in pairs]
    vis = [ph_ref[0, :, 2 * W_HGRN + p * HP:2 * W_HGRN + (p + 1) * HP] for p in pairs]
    vibs = [vi.astype(BF16) for vi in vis]
    sts = [st_s[p] for p in pairs]

    lfs, kks, bs = [], [], []
    for p in pairs:
        xf = ph_ref[0, :, W_HGRN + p * HP:W_HGRN + (p + 1) * HP]
        a = llb_ref[:, sls[p]]
        cc = l1m_ref[:, sls[p]] + _log_sigmoid(xf)
        mx = jnp.maximum(a, cc)
        lfs.append(mx + jnp.log(jnp.exp(a - mx) + jnp.exp(cc - mx)))
        kks.append(jnp.exp(l1m_ref[:, sls[p]]) / (1.0 + jnp.exp(xf)))
    for p in pairs:
        bs.append(_cumsum_rows(tri_l, lfs[p]))

    refs, outs = [], []
    for p in pairs:
        q, kk, b = qs[p], kks[p], bs[p]
        refs.append(sum(jnp.dot(sel_ref[...], part, preferred_element_type=F32) for part in _split3(b)))
        outs.append(_dot_nt((q * jnp.exp(b)).astype(BF16), sts[p].astype(BF16))
                    + head_sum(q * kk) * vis[p])
        b_last = b[L - 1:L]
        kd = kk * jnp.exp(b_last - b)
        st_s[p] = sts[p] * jnp.exp(b_last) + blk * _dot_tn(vibs[p], kd.astype(BF16))

    a01s = [jnp.zeros((2 * L, L), F32) for _ in pairs]
    for lev in range(NL):
        for p in pairs:
            q, kk, b = qs[p], kks[p], bs[p]
            ref = refs[p][lev * L:(lev + 1) * L]
            q_t = q * jnp.exp(jnp.minimum(b - ref, 0.0))
            k_t = (kk * jnp.exp(jnp.minimum(ref - b, 0.0))).astype(BF16)
            q_01 = jnp.concatenate([jnp.where(head0, q_t, 0.0), jnp.where(head0, 0.0, q_t)], axis=0)
            a01s[p] = a01s[p] + msk_ref[lev] * _dot_nt(q_01.astype(BF16), k_t)
    for p in pairs:
        av = jnp.dot(a01s[p].astype(BF16), vibs[p], preferred_element_type=F32)
        outs[p] = outs[p] + jnp.where(head0, av[:L], av[L:])
    for p in pairs:
        o = outs[p]
        gg = ph_ref[0, :, 3 * W_HGRN + p * HP:3 * W_HGRN + (p + 1) * HP]
        y = o * lax.rsqrt(head_sum(o * o) * (1.0 / HEAD_DIM) + EPS) * norm_ref[:, sls[p]]
        y = y * (gg * _sigmoid(gg))
        y_ref[0, :, sls[p]] = y.astype(y_ref.dtype)

    @pl.when(t == pl.num_programs(1) - 1)
    def _():
        s_ref[0] = st_s[...]


def _hgrn(ph, log_lb, log1m_lb, norm, s0, L):
    B, T, _ = ph.shape
    HP = 2 * HEAD_DIM
    NP = H_HGRN // 2
    sel, msk = _hgrn_levels(L)
    NL = msk.shape[0]
    s0t = jnp.swapaxes(s0, -1, -2).reshape(B, NP, 2, HEAD_DIM, HEAD_DIM)
    s0bd = jnp.einsum('bpavk,ac->bpavck', s0t, jnp.eye(2, dtype=F32)).reshape(B, NP, HP, HP)
    tok = lambda b, t: (b, t, 0)
    per_b4 = lambda b, t: (b, 0, 0, 0)
    const = lambda b, t: (0, 0)
    const3 = lambda b, t: (0, 0, 0)
    y, st = pl.pallas_call(
        functools.partial(_hgrn_kernel, L=L, NL=NL),
        grid=(B, T // L),
        in_specs=[pl.BlockSpec((1, L, 4 * W_HGRN), tok),
                  pl.BlockSpec((1, W_HGRN), const),
                  pl.BlockSpec((1, W_HGRN), const),
                  pl.BlockSpec((1, W_HGRN), const),
                  pl.BlockSpec((1, NP, HP, HP), per_b4),
                  pl.BlockSpec((NL * L, L), const),
                  pl.BlockSpec((NL, 2 * L, L), const3)],
        out_specs=[pl.BlockSpec((1, L, W_HGRN), tok),
                   pl.BlockSpec((1, NP, HP, HP), per_b4)],
        out_shape=[jax.ShapeDtypeStruct((B, T, W_HGRN), BF16),
                   jax.ShapeDtypeStruct((B, NP, HP, HP), F32)],
        scratch_shapes=[pltpu.VMEM((NP, HP, HP), F32)],
        compiler_params=_cparams(("arbitrary", "arbitrary")),
        name="hgrn2",
    )(ph, log_lb, log1m_lb, norm, s0bd, sel, msk)
    st6 = st.reshape(B, NP, 2, HEAD_DIM, 2, HEAD_DIM)
    st_heads = jnp.stack([st6[:, :, 0, :, 0, :], st6[:, :, 1, :, 1, :]], axis=2)
    return y, jnp.swapaxes(st_heads.reshape(B, H_HGRN, HEAD_DIM, HEAD_DIM), -1, -2)


def _topk_mask(gate, n_valid, lane, n_max):
    cnt = jnp.zeros(gate.shape, F32)
    for n in range(n_max):
        col = gate[:, n:n + 1]
        ahead = jnp.where(lane > n, jnp.where(col >= gate, 1.0, 0.0), jnp.where(col > gate, 1.0, 0.0))
        cnt = cnt + ahead * jnp.where(n < n_valid, 1.0, 0.0)
    return jnp.where(cnt < MOBA_TOPK, jnp.where(lane < n_valid, 1.0, 0.0), 0.0)


def _moba_prompt_kernel(q_ref, k_ref, v_ref, y_ref, km_s, kb_s, vt_s, s_s, *, NB):
    j = pl.program_id(2)
    BS = MOBA_BLOCK

    @pl.when(j == 0)
    def _():
        km_s[...] = jnp.zeros(km_s.shape, F32)
        for n in range(NB):
            kblk = k_ref[0, n * BS:(n + 1) * BS, :]
            km_s[n:n + 1, :] = jnp.mean(kblk, axis=0, keepdims=True)
            kb_s[n] = kblk.astype(BF16)
            vt_s[n] = jnp.transpose(v_ref[0, n * BS:(n + 1) * BS, :]).astype(BF16)

    sub = lax.broadcasted_iota(jnp.int32, (8, BS), 0)
    ki = lax.broadcasted_iota(jnp.int32, (BS, BS), 0)
    qi = lax.broadcasted_iota(jnp.int32, (BS, BS), 1)

    def tile(cap):
        sls = [slice(h * HEAD_DIM, (h + 1) * HEAD_DIM) for h in range(2)]
        sels, ms = [], []
        for h in range(2):
            sl = sls[h]
            q = q_ref[0, :, sl]
            gate = lax.dot_general(km_s[:, sl], q, (((1,), (1,)), ((), ())),
                                   precision=lax.Precision.HIGHEST, preferred_element_type=F32)
            cnt = jnp.zeros((8, BS), F32)
            for n in range(cap):
                row = gate[n:n + 1, :]
                ahead = jnp.where(sub > n, jnp.where(row >= gate, 1.0, 0.0), jnp.where(row > gate, 1.0, 0.0))
                cnt = cnt + ahead * jnp.where(n < j, 1.0, 0.0)
            sel = jnp.where(cnt < MOBA_TOPK, jnp.where(sub < j, 1.0, 0.0), 0.0)
            qb = (q * (HEAD_DIM ** -0.5 * math.log2(math.e))).astype(BF16)

            s = jnp.where(ki <= qi, _dot_nt(kb_s[j, :, sl], qb), NEG_BIG)
            s_s[h, cap] = s
            m = jnp.max(s, axis=0, keepdims=True)
            for n in range(cap):
                s = _dot_nt(kb_s[n, :, sl], qb)
                s_s[h, n] = s
                m = jnp.maximum(m, jnp.where(sel[n:n + 1, :] > 0.0, jnp.max(s, axis=0, keepdims=True), NEG_BIG))
            sels.append(sel)
            ms.append(m)
        outs = []
        for h in range(2):
            sl, sel, m = sls[h], sels[h], ms[h]
            l = jnp.zeros((1, BS), F32)
            acc = jnp.zeros((HEAD_DIM, BS), F32)
            for n in range(cap + 1):
                off = jnp.where(sel[n:n + 1, :] > 0.0, m, -NEG_BIG) if n < cap else m
                p = jnp.exp2(s_s[h, n] - off)
                l = l + jnp.sum(p, axis=0, keepdims=True)
                vt = vt_s[n, sl, :] if n < cap else vt_s[j, sl, :]
                acc = acc + jnp.dot(vt, p.astype(BF16), preferred_element_type=F32)
            outs.append(acc / l)
        y_ref[0] = jnp.transpose(jnp.concatenate(outs, axis=0)).astype(y_ref.dtype)

    for cap in range(NB):
        pl.when(j == cap)(functools.partial(tile, cap))


def _moba_prompt(q, k, v):
    B, T, _ = q.shape
    NB = T // MOBA_BLOCK
    assert NB <= 8
    HP = 2 * HEAD_DIM
    return pl.pallas_call(
        functools.partial(_moba_prompt_kernel, NB=NB),
        grid=(B, W_MOBA // HP, NB),
        in_specs=[pl.BlockSpec((1, MOBA_BLOCK, HP), lambda b, h, j: (b, j, h)),
                  pl.BlockSpec((1, T, HP), lambda b, h, j: (b, 0, h)),
                  pl.BlockSpec((1, T, HP), lambda b, h, j: (b, 0, h))],
        out_specs=pl.BlockSpec((1, MOBA_BLOCK, HP), lambda b, h, j: (b, j, h)),
        out_shape=jax.ShapeDtypeStruct((B, T, W_MOBA), BF16),
        scratch_shapes=[pltpu.VMEM((8, HP), F32),
                        pltpu.VMEM((NB, MOBA_BLOCK, HP), BF16),
                        pltpu.VMEM((NB, HP, MOBA_BLOCK), BF16),
                        pltpu.VMEM((2, NB + 1, MOBA_BLOCK, MOBA_BLOCK), F32)],
        compiler_params=_cparams(("arbitrary", "arbitrary", "arbitrary")),
        name="moba_prompt",
    )(q, k, v)


PAGES_PER_STEP = 16


def _moba_sample_kernel(pt_ref, q_ref, kn_ref, vn_ref, ck_hbm, cv_hbm, y_ref,
                        kbuf, vbuf, sem, gate_s, m_s, l_s, o_s, *, layer, PPS):
    b = pl.program_id(0)
    c = pl.program_id(1)
    n_chunks = pl.num_programs(1)
    step = b * n_chunks + c
    total = pl.num_programs(0) * n_chunks
    slot = step % 2
    nb_per = PPS // 2
    R = q_ref.shape[1]

    def page_copies(bb, cc, sl):
        out = []
        for p in range(PPS):
            page = pt_ref[bb, cc * PPS + p]
            out.append(pltpu.make_async_copy(ck_hbm.at[page, layer], kbuf.at[sl, p], sem.at[0, sl]))
            out.append(pltpu.make_async_copy(cv_hbm.at[page, layer], vbuf.at[sl, p], sem.at[1, sl]))
        return out

    @pl.when(step == 0)
    def _():
        for cp in page_copies(b, c, slot):
            cp.start()

    @pl.when(step + 1 < total)
    def _():
        nxt = step + 1
        for cp in page_copies(nxt // n_chunks, nxt % n_chunks, 1 - slot):
            cp.start()

    for cp in page_copies(b, c, slot):
        cp.wait()

    @pl.when(c == 0)
    def _():
        gate_s[...] = jnp.zeros(gate_s.shape, F32)
        m_s[...] = jnp.zeros(m_s.shape, F32)
        l_s[...] = jnp.zeros(l_s.shape, F32)

    q_s = q_ref[0] * (HEAD_DIM ** -0.5)
    diag = (lax.broadcasted_iota(jnp.int32, (R, W_MOBA), 0) % H_MOBA
            == lax.broadcasted_iota(jnp.int32, (R, W_MOBA), 1) // HEAD_DIM)
    qbd = jnp.where(diag, jnp.concatenate([q_s] * H_MOBA, axis=1), 0.0)
    q_hi = qbd.astype(BF16)
    q_lo = (qbd - q_hi.astype(F32)).astype(BF16)
    q2 = jnp.concatenate([q_hi, q_lo], axis=0)
    lane = lax.broadcasted_iota(jnp.int32, (R, 128), 1)
    half = MOBA_BLOCK // 2

    gate, m_run, l_run = gate_s[...], m_s[...], l_s[...]
    s_pages = []
    for pg in range(PPS):
        kt = kbuf[slot, pg].reshape(W_MOBA, PAGE_SIZE).astype(BF16)
        s2 = jnp.dot(q2, kt, preferred_element_type=F32)
        s_pages.append(s2[:R] + s2[R:])
    probs = []
    for i in range(nb_per):
        s = jnp.concatenate(s_pages[2 * i:2 * i + 2], axis=1)
        gcol = jnp.sum(s, axis=-1, keepdims=True)
        mcol = jnp.max(s, axis=-1, keepdims=True)
        p = jnp.exp(s - mcol)
        lcol = jnp.sum(p, axis=-1, keepdims=True)
        probs.append(p.astype(BF16))
        hit = lane == c * nb_per + i
        gate = jnp.where(hit, gcol, gate)
        m_run = jnp.where(hit, mcol, m_run)
        l_run = jnp.where(hit, lcol, l_run)
    for i in range(nb_per):
        pb = probs[i]
        o_s[c * nb_per + i] = (
            _dot_nt(pb[:, :half], vbuf[slot, 2 * i].reshape(W_MOBA, PAGE_SIZE).astype(BF16))
            + _dot_nt(pb[:, half:], vbuf[slot, 2 * i + 1].reshape(W_MOBA, PAGE_SIZE).astype(BF16)))
    gate_s[...] = gate
    m_s[...] = m_run
    l_s[...] = l_run

    @pl.when(c == n_chunks - 1)
    def _():
        n_blk = o_s.shape[0]
        sel = _topk_mask(gate_s[...], n_blk, lane, n_blk) > 0.0
        ri = lax.broadcasted_iota(jnp.int32, (R, R), 0)
        ci = lax.broadcasted_iota(jnp.int32, (R, R), 1)
        own_ok = jnp.logical_and(ri % H_MOBA == ci % H_MOBA, ci // H_MOBA <= ri // H_MOBA)
        s_own = lax.dot_general(q_s, kn_ref[0], (((1,), (1,)), ((), ())),
                                precision=lax.Precision.HIGHEST, preferred_element_type=F32)
        s_own = jnp.where(own_ok, s_own, NEG_BIG)
        m_tot = jnp.maximum(jnp.max(jnp.where(sel, m_s[...], NEG_BIG), axis=-1, keepdims=True),
                            jnp.max(s_own, axis=-1, keepdims=True))
        w = jnp.where(sel, jnp.exp(m_s[...] - m_tot), 0.0)
        p_own = jnp.exp(s_own - m_tot)
        den = jnp.sum(w * l_s[...], axis=-1, keepdims=True) + jnp.sum(p_own, axis=-1, keepdims=True)
        wide = jnp.zeros((R, W_MOBA), F32)
        for n in range(n_blk):
            wide = wide + w[:, n:n + 1] * o_s[n]
        wide = jnp.where(diag, wide, 0.0)
        acc = lax.dot_general(p_own, vn_ref[0], (((1,), (0,)), ((), ())),
                              precision=lax.Precision.HIGHEST, preferred_element_type=F32)
        for h in range(H_MOBA):
            acc = acc + wide[:, h * HEAD_DIM:(h + 1) * HEAD_DIM]
        y_ref[0] = (acc / den).astype(y_ref.dtype)


def _moba_sample(q, kn, vn, cache_k, cache_v, page_table, layer):
    B, R, _ = q.shape
    n_pages = page_table.shape[1]
    PPS = min(PAGES_PER_STEP, n_pages)
    n_chunks = n_pages // PPS
    n_blk = n_pages * PAGE_SIZE // MOBA_BLOCK
    assert n_blk <= 128
    per_b = lambda b, c, pt: (b, 0, 0)
    grid_spec = pltpu.PrefetchScalarGridSpec(
        num_scalar_prefetch=1,
        grid=(B, n_chunks),
        in_specs=[pl.BlockSpec((1, R, HEAD_DIM), per_b),
                  pl.BlockSpec((1, R, HEAD_DIM), per_b),
                  pl.BlockSpec((1, R, HEAD_DIM), per_b),
                  pl.BlockSpec(memory_space=pl.ANY),
                  pl.BlockSpec(memory_space=pl.ANY)],
        out_specs=pl.BlockSpec((1, R, HEAD_DIM), per_b),
        scratch_shapes=[pltpu.VMEM((2, PPS, H_MOBA, HEAD_DIM, PAGE_SIZE), F32),
                        pltpu.VMEM((2, PPS, H_MOBA, HEAD_DIM, PAGE_SIZE), F32),
                        pltpu.SemaphoreType.DMA((2, 2)),
                        pltpu.VMEM((R, 128), F32),
                        pltpu.VMEM((R, 128), F32),
                        pltpu.VMEM((R, 128), F32),
                        pltpu.VMEM((n_blk, R, W_MOBA), F32)])
    return pl.pallas_call(
        functools.partial(_moba_sample_kernel, layer=layer, PPS=PPS),
        grid_spec=grid_spec,
        out_shape=jax.ShapeDtypeStruct((B, R, HEAD_DIM), BF16),
        compiler_params=_cparams(("arbitrary", "arbitrary")),
        name="moba_sample",
    )(page_table, q, kn, vn, cache_k, cache_v)


def _outproj_kernel(x_ref, ya_ref, yb_ref, yc_ref, w_ref, g_ref, o_ref):
    a0, a1 = W_MLSTM, W_MLSTM + W_MOBA
    mix = (jnp.dot(ya_ref[...], w_ref[0:a0, :], preferred_element_type=F32)
           + jnp.dot(yb_ref[...], w_ref[a0:a1, :], preferred_element_type=F32)
           + jnp.dot(yc_ref[...], w_ref[a1:, :], preferred_element_type=F32))
    o_ref[...] = x_ref[...] + mix * lax.rsqrt(jnp.mean(mix * mix, axis=-1, keepdims=True) + EPS) * g_ref[...]


def _outproj(x2d, ya, yb, yc, w, g, tm):
    n = x2d.shape[0]
    row = lambda i: (i, 0)
    const = lambda i: (0, 0)
    return pl.pallas_call(
        _outproj_kernel,
        grid=(n // tm,),
        in_specs=[pl.BlockSpec((tm, D_MODEL), row),
                  pl.BlockSpec((tm, W_MLSTM), row),
                  pl.BlockSpec((tm, W_MOBA), row),
                  pl.BlockSpec((tm, W_HGRN), row),
                  pl.BlockSpec((D_MODEL, D_MODEL), const),
                  pl.BlockSpec((1, D_MODEL), const)],
        out_specs=pl.BlockSpec((tm, D_MODEL), row),
        out_shape=jax.ShapeDtypeStruct((n, D_MODEL), F32),
        compiler_params=_cparams(("arbitrary",)),
        name="outproj",
    )(x2d, ya, yb, yc, w, g)


FF_CHUNKS = 2
FF_CHUNK = D_FF // FF_CHUNKS


def _gelu_tanh(x):
    return 0.5 * x * (1.0 + jnp.tanh(math.sqrt(2.0 / math.pi) * (x + 0.044715 * (x * x * x))))


def _ffn_kernel(x_ref, g2_ref, wu_ref, wg_ref, wd_ref, cw_ref, cb_ref, c0_ref, g3_ref,
                o_ref, tail_ref, hb_s, acc_s, ext_s, *, tm, R, S):
    tt = pl.program_id(1)
    c = pl.program_id(2)

    @pl.when(c == 0)
    def _():
        x = x_ref[...]
        h = x * lax.rsqrt(jnp.mean(x * x, axis=-1, keepdims=True) + EPS) * g2_ref[...]
        hb_s[...] = h.astype(BF16)

    @pl.when(tt == 0)
    def _():
        ext_s[c, 0:R, :] = c0_ref[0]

    hb = hb_s[...]
    u = jnp.dot(hb, wu_ref[...], preferred_element_type=F32)
    g = jnp.dot(hb, wg_ref[...], preferred_element_type=F32)
    ext_s[c, R:R + tm, :] = u
    uc = (cb_ref[...]
          + cw_ref[0:1, :] * ext_s[c, R - 2 * S:R - 2 * S + tm, :]
          + cw_ref[1:2, :] * ext_s[c, R - S:R - S + tm, :]
          + cw_ref[2:3, :] * u)
    act = (_gelu_tanh(uc) * g).astype(BF16)
    d = jnp.dot(act, wd_ref[...], preferred_element_type=F32)

    @pl.when(c == 0)
    def _():
        acc_s[...] = d

    @pl.when(c > 0)
    def _():
        acc_s[...] += d

    tail = ext_s[c, tm:tm + R, :]
    tail_ref[0, c] = tail
    ext_s[c, 0:R, :] = tail

    @pl.when(c == pl.num_programs(2) - 1)
    def _():
        a = acc_s[...]
        o_ref[...] = x_ref[...] + a * lax.rsqrt(jnp.mean(a * a, axis=-1, keepdims=True) + EPS) * g3_ref[...]


def _ffn(x2d, g2, w_up, w_down, conv_w8, conv_b, carry0, g3, *, nseq, tm, R, S):
    n = x2d.shape[0]
    ntt = n // (nseq * tm)
    row = lambda s, t, c: (s * ntt + t, 0)
    const = lambda s, t, c: (0, 0)
    x_new, tail = pl.pallas_call(
        functools.partial(_ffn_kernel, tm=tm, R=R, S=S),
        grid=(nseq, ntt, FF_CHUNKS),
        in_specs=[pl.BlockSpec((tm, D_MODEL), row),
                  pl.BlockSpec((1, D_MODEL), const),
                  pl.BlockSpec((D_MODEL, FF_CHUNK), lambda s, t, c: (0, c)),
                  pl.BlockSpec((D_MODEL, FF_CHUNK), lambda s, t, c: (0, FF_CHUNKS + c)),
                  pl.BlockSpec((FF_CHUNK, D_MODEL), lambda s, t, c: (c, 0)),
                  pl.BlockSpec((8, FF_CHUNK), lambda s, t, c: (0, c)),
                  pl.BlockSpec((1, FF_CHUNK), lambda s, t, c: (0, c)),
                  pl.BlockSpec((1, R, FF_CHUNK), lambda s, t, c: (s, 0, c)),
                  pl.BlockSpec((1, D_MODEL), const)],
        out_specs=[pl.BlockSpec((tm, D_MODEL), row),
                   pl.BlockSpec((1, FF_CHUNKS, R, FF_CHUNK), lambda s, t, c: (s, 0, 0, 0))],
        out_shape=[jax.ShapeDtypeStruct((n, D_MODEL), F32),
                   jax.ShapeDtypeStruct((nseq, FF_CHUNKS, R, FF_CHUNK), F32)],
        scratch_shapes=[pltpu.VMEM((tm, D_MODEL), BF16),
                        pltpu.VMEM((tm, D_MODEL), F32),
                        pltpu.VMEM((FF_CHUNKS, R + tm, FF_CHUNK), F32)],
        compiler_params=_cparams(("arbitrary", "arbitrary", "arbitrary")),
        name="convffn",
    )(x2d, g2, w_up, w_up, w_down, conv_w8, conv_b, carry0, g3)
    return x_new, jnp.transpose(tail, (0, 2, 1, 3)).reshape(nseq, R, D_FF)


def _rope_tables(pos):
    half = HEAD_DIM // 2
    inv = ROPE_THETA ** (-jnp.arange(half, dtype=F32) / half)
    ang = pos.astype(F32)[:, None] * inv[None, :]
    cos = jnp.cos(ang)
    sin = jnp.sin(ang)
    cos_t = jnp.tile(jnp.concatenate([cos, cos], axis=-1), (1, H_MOBA))
    sin_t = jnp.tile(jnp.concatenate([-sin, sin], axis=-1), (1, H_MOBA))
    return cos_t, sin_t


def _layer_weights(l, w_in_r, gate_bias, mlstm_norm, lower_bounds, hgrn_norm, w_out_b, norms,
                   w_up_b, conv_w, conv_b, w_down_b):
    bias = gate_bias[l].reshape(2 * H_MLSTM)
    lb = lower_bounds[l]
    return dict(
        w_in=w_in_r[l], w_out=w_out_b[l], w_up=w_up_b[l], w_down=w_down_b[l],
        g0=norms[l, 0][None], g1=norms[l, 1][None], g2=norms[l, 2][None], g3=norms[l, 3][None],
        bias_row=jnp.pad(bias, (0, GATE_W - 2 * H_MLSTM))[None],
        bias_col=bias[:, None],
        m_norm=mlstm_norm[l][None], h_norm=hgrn_norm[l][None],
        log_lb=jnp.log(lb)[None], log1m_lb=jnp.log1p(-lb)[None],
        conv_w8=jnp.pad(conv_w[l], ((0, 8 - CONV_W), (0, 0))), conv_b=conv_b[l][None])


def _layer_prompt(x2d, B, T, lw, cos, sin, k_pages, v_pages, layer):
    pm, q, k, v, ph, gt, k_pages, v_pages = _inproj(x2d, lw["g0"], lw["w_in"], cos, sin, 256,
                                                    pages=(k_pages, v_pages, layer))
    r3 = lambda a: a.reshape(B, T, a.shape[-1])
    gates = r3(gt)
    gates_t = jnp.transpose(gates[:, :, :2 * H_MLSTM], (0, 2, 1))
    c0 = jnp.zeros((B, H_MLSTM, HEAD_DIM, HEAD_DIM), F32)
    n0 = jnp.zeros((B, H_MLSTM, HEAD_DIM), F32)
    m0 = jnp.zeros((B, 1, GATE_W), F32)
    ya, c_new, n_new, m_new = _mlstm(r3(pm), gates, gates_t, lw["bias_row"], lw["bias_col"],
                                     lw["m_norm"], c0, n0, m0, min(128, T))
    yb = _moba_prompt(r3(q), r3(k), r3(v))
    s0 = jnp.zeros((B, H_HGRN, HEAD_DIM, HEAD_DIM), F32)
    yc, s_new = _hgrn(r3(ph), lw["log_lb"], lw["log1m_lb"], lw["h_norm"], s0, min(128, T))
    r2 = lambda a: a.reshape(B * T, a.shape[-1])
    x1 = _outproj(x2d, r2(ya), r2(yb), r2(yc), lw["w_out"], lw["g1"], 512)
    carry0 = jnp.zeros((B, 8, D_FF), F32)
    x2, tail = _ffn(x1, lw["g2"], lw["w_up"], lw["w_down"], lw["conv_w8"], lw["conv_b"], carry0,
                    lw["g3"], nseq=B, tm=min(512, T), R=8, S=1)
    return (x2, k_pages, v_pages, c_new, n_new, m_new[:, 0, :H_MLSTM], s_new, tail[:, 8 - (CONV_W - 1):])


SAMPLE_CHUNK = 8
SAMPLE_CHUNK_MLSTM = 128


def _layer_sample(x2d, B, T, lw, cos, sin, cache_k, cache_v, page_table, layer, c0, n0, m0, s0, conv0):
    pm, q, k, v, ph, gt = _inproj(x2d, lw["g0"], lw["w_in"], cos, sin, B * T)
    L = SAMPLE_CHUNK

    def seq_major(a, pad_row=None, rows=L):
        a = jnp.transpose(a.reshape(T, B, a.shape[-1]), (1, 0, 2))
        if pad_row is None:
            return a
        fill = jnp.broadcast_to(pad_row[None, None, :], (B, rows - T, a.shape[-1]))
        return jnp.concatenate([a, fill], axis=1)

    def time_major(a):
        return jnp.transpose(a[:, :T], (1, 0, 2)).reshape(T * B, a.shape[-1])

    gate_pad = jnp.where(jnp.arange(GATE_W) < H_MLSTM, NEG_BIG, -NEG_BIG).astype(F32)
    LM = SAMPLE_CHUNK_MLSTM
    gates = seq_major(gt, gate_pad, LM)
    gates_t = jnp.transpose(gates[:, :, :2 * H_MLSTM], (0, 2, 1))
    m0p = jnp.pad(m0, ((0, 0), (0, GATE_W - H_MLSTM)))[:, None, :]
    ya, c_new, n_new, m_new = _mlstm(seq_major(pm, jnp.zeros((4 * W_MLSTM,), F32), LM), gates, gates_t,
                                     lw["bias_row"], lw["bias_col"], lw["m_norm"], c0, n0, m0p, LM)
    col = jnp.arange(4 * W_HGRN)
    h_pad = jnp.where((col >= W_HGRN) & (col < 2 * W_HGRN), -NEG_BIG, 0.0).astype(F32)
    yc, s_new = _hgrn(seq_major(ph, h_pad), lw["log_lb"], lw["log1m_lb"], lw["h_norm"], s0, L)
    heads = lambda a: seq_major(a).reshape(B, T * H_MOBA, HEAD_DIM)
    yb = _moba_sample(heads(q), heads(k), heads(v), cache_k, cache_v, page_table, layer)
    yb = yb.reshape(B, T, W_MOBA)
    x1 = _outproj(x2d, time_major(ya), time_major(yb), time_major(yc), lw["w_out"], lw["g1"], B * T)
    carry0 = jnp.transpose(conv0, (1, 0, 2)).reshape(1, (CONV_W - 1) * B, D_FF)
    x2, tail = _ffn(x1, lw["g2"], lw["w_up"], lw["w_down"], lw["conv_w8"], lw["conv_b"], carry0,
                    lw["g3"], nseq=1, tm=B * T, R=(CONV_W - 1) * B, S=B)
    conv_new = jnp.transpose(tail.reshape(CONV_W - 1, B, D_FF), (1, 0, 2))
    tb = lambda a: jnp.transpose(a.reshape(T, B, H_MOBA, HEAD_DIM), (1, 0, 2, 3))
    return (x2, tb(k), tb(v), c_new, n_new, m_new[:, 0, :H_MLSTM], s_new, conv_new)


def kernel(x_prompt, x_sample, cache_k, cache_v, page_table, state_mlstm_c, state_mlstm_n,
           state_mlstm_m, state_hgrn, state_conv, w_in, mlstm_gate_bias, mlstm_norm,
           hgrn_lb_logits, hgrn_norm, w_out, norms, ffn_w_up, ffn_conv_w, ffn_conv_b, ffn_w_down):
    Bp, Tp, _ = x_prompt.shape
    Bs, Ts, _ = x_sample.shape
    depth = w_in.shape[0]
    past_len = page_table.shape[1] * PAGE_SIZE

    lb_cs = jnp.cumsum(jax.nn.softmax(hgrn_lb_logits.astype(F32), axis=0), axis=0)
    lower_bounds = lb_cs - lb_cs[0:1]
    g0 = 4 * W_MLSTM
    g1 = g0 + 2 * H_MLSTM
    w_in_r = jnp.concatenate(
        [w_in[:, :, :g0], w_in[:, :, g1:], w_in[:, :, g0:g1],
         jnp.zeros((depth, D_MODEL, GATE_W - 2 * H_MLSTM), w_in.dtype)], axis=-1).astype(BF16)
    w_out_b = w_out.astype(BF16)
    w_up_b = ffn_w_up.astype(BF16)
    w_down_b = ffn_w_down.astype(BF16)
    cos_p, sin_p = _rope_tables(jnp.arange(Tp))
    cos_s, sin_s = _rope_tables(past_len + jnp.arange(Ts * Bs) // Bs)

    cache_k = jnp.transpose(cache_k, (0, 1, 3, 4, 2))
    cache_v = jnp.transpose(cache_v, (0, 1, 3, 4, 2))

    n_pg = Tp // PAGE_SIZE
    k_pages = jnp.zeros((Bp * n_pg, depth, H_MOBA, HEAD_DIM, PAGE_SIZE), F32)
    v_pages = jnp.zeros((Bp * n_pg, depth, H_MOBA, HEAD_DIM, PAGE_SIZE), F32)

    xp = x_prompt.reshape(Bp * Tp, D_MODEL)
    xs = jnp.transpose(x_sample, (1, 0, 2)).reshape(Ts * Bs, D_MODEL)
    rec_p, rec_s = [], []
    for l in range(depth):
        lw = _layer_weights(l, w_in_r, mlstm_gate_bias, mlstm_norm, lower_bounds, hgrn_norm,
                            w_out_b, norms, w_up_b, ffn_conv_w, ffn_conv_b, w_down_b)
        xp, *new_p = _layer_prompt(xp, Bp, Tp, lw, cos_p, sin_p, k_pages, v_pages, l)
        k_pages, v_pages = new_p[0], new_p[1]
        xs, *new_s = _layer_sample(xs, Bs, Ts, lw, cos_s, sin_s, cache_k, cache_v, page_table, l,
                                   state_mlstm_c[l], state_mlstm_n[l], state_mlstm_m[l],
                                   state_hgrn[l], state_conv[l])
        rec_p.append(new_p)
        rec_s.append(new_s)

    def stack(rec, i, axis):
        return jnp.stack([r[i] for r in rec], axis=axis)

    paged = lambda a: jnp.transpose(a.reshape(Bp, n_pg, depth, H_MOBA, HEAD_DIM, PAGE_SIZE), (0, 1, 2, 5, 3, 4))
    k_prompt = paged(k_pages)
    v_prompt = paged(v_pages)
    y_prompt = xp.reshape(Bp, Tp, D_MODEL)
    y_sample = jnp.transpose(xs.reshape(Ts, Bs, D_MODEL), (1, 0, 2))
    return (y_prompt, y_sample, k_prompt, v_prompt,
            stack(rec_p, 2, 0), stack(rec_p, 3, 0), stack(rec_p, 4, 0), stack(rec_p, 5, 0),
            stack(rec_p, 6, 0),
            stack(rec_s, 0, 1), stack(rec_s, 1, 1),
            stack(rec_s, 2, 0), stack(rec_s, 3, 0), stack(rec_s, 4, 0), stack(rec_s, 5, 0),
            stack(rec_s, 6, 0))
```
